```python
import jax, jax.numpy as jnp
from jax import lax
import numpy as np

D_MODEL = 1024
BATCH = 8
SEQ = 4096
DEPTH = 1

HA_HEADS = 8
HA_DK = 128
HA_DV = 128
CHUNK = 64
LRU_WIDTH = 1024
LRU_HEADS = 4
LRU_BLOCK = LRU_WIDTH // LRU_HEADS
CONV_WIDTH = 4
RG_C = 8.0
N_EXPERTS = 32
TOP_K = 4
D_FF = 1024
SWIGLU_LIMIT = 7.0
SWIGLU_ALPHA = 1.702
EPS = 1e-5

_IN_SIZES = (HA_HEADS * HA_DK, HA_HEADS * HA_DK, HA_HEADS * HA_DV, HA_HEADS * HA_DV,
             LRU_WIDTH, LRU_WIDTH, D_MODEL, D_MODEL)
IN_COLS = sum(_IN_SIZES)

kernel_name = "hgrn2_rglru_gated_merge_moe_block"


def _split_points():
    return [int(v) for v in np.cumsum(_IN_SIZES)[:-1]]


def rms_norm(x, w):
    xf = x.astype(jnp.float32)
    y = xf * lax.rsqrt(jnp.mean(xf * xf, axis=-1, keepdims=True) + EPS)
    return (y * w.astype(jnp.float32)).astype(x.dtype)


def _to_chunks(t, d):
    b, s = t.shape[:2]
    return t.reshape(b, s // CHUNK, CHUNK, HA_HEADS, d).transpose(0, 3, 1, 2, 4)


def hgrn2_branch(q, f_logit, i_v, g_out, lb, gn_w):
    b, s = q.shape[:2]
    f = lb + (1.0 - lb) * jax.nn.sigmoid(f_logit.astype(jnp.float32))
    log_f = jnp.log(f)
    k = 1.0 - f
    qs = jax.nn.silu(q.astype(jnp.float32))
    q_c, k_c, lf_c = _to_chunks(qs, HA_DK), _to_chunks(k, HA_DK), _to_chunks(log_f, HA_DK)
    v_c = _to_chunks(i_v.astype(jnp.float32), HA_DV)

    bcum = jnp.cumsum(lf_c, axis=3)
    b_last = bcum[:, :, :, -1:, :]
    q_in = q_c * jnp.exp(bcum)
    k_in = k_c * jnp.exp(-bcum)
    k_dec = k_c * jnp.exp(b_last - bcum)

    causal = jnp.tril(jnp.ones((CHUNK, CHUNK), dtype=bool))
    scores = jnp.einsum('bhnid,bhnjd->bhnij', q_in, k_in)
    scores = jnp.where(causal, scores, 0.0)
    o_intra = jnp.einsum('bhnij,bhnjv->bhniv', scores, v_c)

    def chunk_step(state, inp):
        q_n, kd_n, v_n, dec_n = inp
        o_n = jnp.einsum('bhcd,bhdv->bhcv', q_n, state)
        state = dec_n[..., None] * state + jnp.einsum('bhcd,bhcv->bhdv', kd_n, v_n)
        return state, o_n

    xs = (jnp.moveaxis(q_in, 2, 0), jnp.moveaxis(k_dec, 2, 0), jnp.moveaxis(v_c, 2, 0),
          jnp.moveaxis(jnp.exp(b_last[:, :, :, 0, :]), 2, 0))
    state0 = jnp.zeros((b, HA_HEADS, HA_DK, HA_DV), jnp.float32)
    _, o_inter = lax.scan(chunk_step, state0, xs)
    o = o_intra + jnp.moveaxis(o_inter, 0, 2)
    o = o.transpose(0, 2, 3, 1, 4).reshape(b, s, HA_HEADS, HA_DV)
    o = o * lax.rsqrt(jnp.mean(o * o, axis=-1, keepdims=True) + EPS)
    o = o * gn_w.astype(jnp.float32).reshape(HA_HEADS, HA_DV)
    g = jax.nn.silu(g_out.astype(jnp.float32)).reshape(b, s, HA_HEADS, HA_DV)
    return (o * g).reshape(b, s, HA_HEADS * HA_DV)


def _linear_recurrence(c1, c2):
    a1, u1 = c1
    a2, u2 = c2
    return a1 * a2, a2 * u1 + u2


def rglru_branch(xb, gate_b, conv_w, conv_b, w_rg_a, b_rg_a, w_rg_x, b_rg_x, lru_lambda):
    b, s = xb.shape[:2]
    xf = xb.astype(jnp.float32)
    xpad = jnp.pad(xf, ((0, 0), (CONV_WIDTH - 1, 0), (0, 0)))
    cw = conv_w.astype(jnp.float32)
    xc = conv_b.astype(jnp.float32) + sum(xpad[:, k:k + s, :] * cw[k] for k in range(CONV_WIDTH))
    xh = xc.reshape(b, s, LRU_HEADS, LRU_BLOCK)
    r = jax.nn.sigmoid(jnp.einsum('bshi,hij->bshj', xh, w_rg_a.astype(jnp.float32))
                       + b_rg_a.astype(jnp.float32).reshape(LRU_HEADS, LRU_BLOCK))
    ig = jax.nn.sigmoid(jnp.einsum('bshi,hij->bshj', xh, w_rg_x.astype(jnp.float32))
                        + b_rg_x.astype(jnp.float32).reshape(LRU_HEADS, LRU_BLOCK))
    log_a = RG_C * r * jax.nn.log_sigmoid(lru_lambda.astype(jnp.float32)).reshape(LRU_HEADS, LRU_BLOCK)
    a = jnp.exp(log_a)
    mult = jnp.sqrt(-jnp.expm1(2.0 * log_a))
    first = (jnp.arange(s) == 0)[None, :, None, None]
    mult = jnp.where(first, 1.0, mult)
    u = xh * ig * mult
    _, h = lax.associative_scan(_linear_recurrence, (a, u), axis=1)
    return h.reshape(b, s, LRU_WIDTH) * jax.nn.gelu(gate_b.astype(jnp.float32))


def moe_ffn(h, w_router, b_router, w_gu, b_gu, w_down, b_down):
    b, s, d = h.shape
    ht = h.reshape(b * s, d)
    logits = (ht @ w_router + b_router).astype(jnp.float32)
    top_val, top_idx = lax.top_k(logits, TOP_K)
    top_w = jax.nn.softmax(top_val, axis=-1)
    gate = jnp.einsum('tk,tke->te', top_w, jax.nn.one_hot(top_idx, N_EXPERTS, dtype=jnp.float32))
    out = jnp.zeros((b * s, d), jnp.float32)
    for e in range(N_EXPERTS):
        gu = (ht @ w_gu[e] + b_gu[e]).astype(jnp.float32)
        x_glu = jnp.minimum(gu[:, 0::2], SWIGLU_LIMIT)
        x_lin = jnp.clip(gu[:, 1::2], -SWIGLU_LIMIT, SWIGLU_LIMIT)
        act = x_glu * jax.nn.sigmoid(SWIGLU_ALPHA * x_glu) * (x_lin + 1.0)
        y_e = act.astype(ht.dtype) @ w_down[e] + b_down[e]
        out = out + gate[:, e:e + 1] * y_e.astype(jnp.float32)
    return out.reshape(b, s, d)


def setup_inputs(seed: int = 0) -> dict:
    key = jax.random.key(seed)
    ks = jax.random.split(key, 24)
    f32 = jnp.float32
    HK = HA_HEADS * HA_DK
    HV = HA_HEADS * HA_DV

    def nrm(k, shape, scale):
        return jax.random.normal(k, shape, f32) * scale

    a0 = jax.random.uniform(ks[11], (DEPTH, LRU_WIDTH), f32, 0.9, 0.999)
    s0 = a0 ** (1.0 / RG_C)
    lru_lambda = jnp.log(s0) - jnp.log1p(-s0)
    return {
        "x": nrm(ks[0], (BATCH, SEQ, D_MODEL), 1.0),
        "norm_mix": 1.0 + nrm(ks[1], (DEPTH, D_MODEL), 0.02),
        "w_in": nrm(ks[2], (DEPTH, D_MODEL, IN_COLS), D_MODEL ** -0.5),
        "lb_raw": nrm(ks[3], (DEPTH + 1, HK), 0.1),
        "gn_w": 1.0 + nrm(ks[4], (DEPTH, HV), 0.02),
        "conv_w": nrm(ks[5], (DEPTH, CONV_WIDTH, LRU_WIDTH), CONV_WIDTH ** -0.5),
        "conv_b": nrm(ks[6], (DEPTH, LRU_WIDTH), 0.01),
        "w_rg_a": nrm(ks[7], (DEPTH, LRU_HEADS, LRU_BLOCK, LRU_BLOCK), LRU_BLOCK ** -0.5),
        "b_rg_a": nrm(ks[8], (DEPTH, LRU_WIDTH), 0.01),
        "w_rg_x": nrm(ks[9], (DEPTH, LRU_HEADS, LRU_BLOCK, LRU_BLOCK), LRU_BLOCK ** -0.5),
        "b_rg_x": nrm(ks[10], (DEPTH, LRU_WIDTH), 0.01),
        "lru_lambda": lru_lambda,
        "w_proj_a": nrm(ks[12], (DEPTH, HV, D_MODEL), HV ** -0.5),
        "w_proj_b": nrm(ks[13], (DEPTH, LRU_WIDTH, D_MODEL), LRU_WIDTH ** -0.5),
        "w_out": nrm(ks[14], (DEPTH, D_MODEL, D_MODEL), D_MODEL ** -0.5),
        "norm_ffn": 1.0 + nrm(ks[15], (DEPTH, D_MODEL), 0.02),
        "w_router": nrm(ks[16], (DEPTH, D_MODEL, N_EXPERTS), D_MODEL ** -0.5),
        "b_router": nrm(ks[17], (DEPTH, N_EXPERTS), 0.01),
        "w_gu": nrm(ks[18], (DEPTH, N_EXPERTS, D_MODEL, 2 * D_FF), D_MODEL ** -0.5),
        "b_gu": nrm(ks[19], (DEPTH, N_EXPERTS, 2 * D_FF), 0.01),
        "w_down": nrm(ks[20], (DEPTH, N_EXPERTS, D_FF, D_MODEL), D_FF ** -0.5),
        "b_down": nrm(ks[21], (DEPTH, N_EXPERTS, D_MODEL), 0.01),
        "norm_final": 1.0 + nrm(ks[22], (D_MODEL,), 0.02),
    }


def reference(x, norm_mix, w_in, lb_raw, gn_w, conv_w, conv_b, w_rg_a, b_rg_a, w_rg_x, b_rg_x,
              lru_lambda, w_proj_a, w_proj_b, w_out, norm_ffn, w_router, b_router, w_gu, b_gu,
              w_down, b_down, norm_final):
    lower_bounds = jnp.cumsum(jax.nn.softmax(lb_raw.astype(jnp.float32), axis=0), axis=0)
    split_points = _split_points()
    for l in range(DEPTH):
        h = rms_norm(x, norm_mix[l])
        proj = h @ w_in[l]
        q, f_logit, i_v, g_out, xb, gate_b, g_a, g_b = jnp.split(proj, split_points, axis=-1)
        o_a = hgrn2_branch(q, f_logit, i_v, g_out, lower_bounds[l], gn_w[l])
        o_b = rglru_branch(xb, gate_b, conv_w[l], conv_b[l], w_rg_a[l], b_rg_a[l],
                           w_rg_x[l], b_rg_x[l], lru_lambda[l])
        y_a = o_a.astype(x.dtype) @ w_proj_a[l]
        y_b = o_b.astype(x.dtype) @ w_proj_b[l]
        merged = (jax.nn.sigmoid(g_a.astype(jnp.float32)) * y_a.astype(jnp.float32)
                  + jax.nn.sigmoid(g_b.astype(jnp.float32)) * y_b.astype(jnp.float32))
        x = x + (merged.astype(x.dtype) @ w_out[l]).astype(x.dtype)
        h = rms_norm(x, norm_ffn[l])
        x = x + moe_ffn(h, w_router[l], b_router[l], w_gu[l], b_gu[l], w_down[l], b_down[l]).astype(x.dtype)
    return rms_norm(x, norm_final)
```

```python
import functools

import jax
import jax.numpy as jnp
from jax import lax
from jax.experimental import pallas as pl
from jax.experimental.pallas import tpu as pltpu

F32 = jnp.float32
BF16 = jnp.bfloat16

D_MODEL = 1024
HA_HEADS = 8
HA_DK = 128
HA_DV = 128
CHUNK = 64
LRU_WIDTH = 1024
LRU_HEADS = 4
LRU_BLOCK = LRU_WIDTH // LRU_HEADS
CONV_WIDTH = 4
RG_C = 8.0
N_EXPERTS = 32
TOP_K = 4
D_FF = 1024
SWIGLU_LIMIT = 7.0
SWIGLU_ALPHA = 1.702
EPS = 1e-5

SUBLANES = 8
LANES = 128
TS = 256
TM = 512
TF = 512
VMEM_LIMIT = 56 * 1024 * 1024

NT_DIMS = (((1,), (1,)), ((), ()))
TN_DIMS = (((0,), (0,)), ((), ()))


def _rms(x, w):
    return x * lax.rsqrt(jnp.mean(x * x, axis=-1, keepdims=True) + EPS) * w


def _split3(v):
    hi = v.astype(BF16)
    r1 = v - hi.astype(F32)
    mid = r1.astype(BF16)
    lo = (r1 - mid.astype(F32)).astype(BF16)
    return hi, mid, lo


def _hgrn_kernel(x_ref, nw_ref, wa_ref, wga_ref, lbraw_ref, gnw_ref, wpa_ref, tri_ref, out_ref,
                 st_ref, qin_ref, kin_ref, kdec_ref, v_ref, g_ref, oa_ref):
    n_chunks = TS // CHUNK
    hk = HA_HEADS * HA_DK

    @pl.when(pl.program_id(1) == 0)
    def _():
        st_ref[...] = jnp.zeros_like(st_ref)

    hb = _rms(x_ref[0], nw_ref[...]).astype(BF16)

    lbr = lbraw_ref[...]
    rows = [lbr[j:j + 1, :] for j in range(lbr.shape[0])]
    mx = functools.reduce(jnp.maximum, rows)
    es = [jnp.exp(r - mx) for r in rows]
    lb = es[0] / functools.reduce(lambda a, b: a + b, es)

    fl = jnp.dot(hb, wa_ref[:, hk:2 * hk], preferred_element_type=F32)
    f = lb + (1.0 - lb) * jax.nn.sigmoid(fl)
    lf = jnp.log(f)
    k = 1.0 - f
    tri = tri_ref[...]
    bcum = functools.reduce(lambda a, b: a + b,
                            [jnp.dot(tri, p, preferred_element_type=F32) for p in _split3(lf)])
    b_last = [bcum[c * CHUNK + CHUNK - 1:c * CHUNK + CHUNK, :] for c in range(n_chunks)]
    bl_full = jnp.concatenate([jnp.broadcast_to(b, (CHUNK, hk)) for b in b_last], axis=0)

    q = jnp.dot(hb, wa_ref[:, 0:hk], preferred_element_type=F32)
    qin_ref[...] = (jax.nn.silu(q) * jnp.exp(bcum)).astype(BF16)
    kin_ref[...] = (k * jnp.exp(-bcum)).astype(BF16)
    kdec_ref[...] = (k * jnp.exp(bl_full - bcum)).astype(BF16)
    v_ref[...] = jnp.dot(hb, wa_ref[:, 2 * hk:3 * hk], preferred_element_type=F32).astype(BF16)
    g = jnp.dot(hb, wa_ref[:, 3 * hk:4 * hk], preferred_element_type=F32)
    g_ref[...] = jax.nn.silu(g) * gnw_ref[...]

    causal = (lax.broadcasted_iota(jnp.int32, (CHUNK, CHUNK), 0)
              >= lax.broadcasted_iota(jnp.int32, (CHUNK, CHUNK), 1))
    for c in range(n_chunks):
        r = slice(c * CHUNK, (c + 1) * CHUNK)
        edec = jnp.exp(b_last[c])
        for h in range(HA_HEADS):
            l = slice(h * HA_DK, (h + 1) * HA_DK)
            qh, kh, kd, vh = qin_ref[r, l], kin_ref[r, l], kdec_ref[r, l], v_ref[r, l]
            s = lax.dot_general(qh, kh, NT_DIMS, preferred_element_type=F32)
            s = jnp.where(causal, s, 0.0).astype(BF16)
            st = st_ref[h]
            o = (jnp.dot(s, vh, preferred_element_type=F32)
                 + lax.dot_general(qh, st.astype(BF16), NT_DIMS, preferred_element_type=F32))
            st_ref[h] = st * edec[:, l] + lax.dot_general(vh, kd, TN_DIMS, preferred_element_type=F32)
            ms = jnp.mean(o * o, axis=-1, keepdims=True)
            oa_ref[r, l] = (o * lax.rsqrt(ms + EPS) * g_ref[r, l]).astype(BF16)

    ya = jnp.dot(oa_ref[...], wpa_ref[...], preferred_element_type=F32)
    ga = jnp.dot(hb, wga_ref[...], preferred_element_type=F32)
    out_ref[0] = jax.nn.sigmoid(ga) * ya


def _hgrn_call(x, nw, wa, wga, lb_raw, gnw, wpa, tri):
    b, s, d = x.shape
    hk = HA_HEADS * HA_DK
    const = lambda shape: pl.BlockSpec(shape, lambda i, j: (0,) * len(shape))
    return pl.pallas_call(
        _hgrn_kernel,
        grid=(b, s // TS),
        in_specs=[
            pl.BlockSpec((1, TS, d), lambda i, j: (i, j, 0)),
            const((1, d)), const((d, 4 * hk)), const((d, d)), const(lb_raw.shape), const((1, hk)),
            const((hk, d)), const((TS, TS)),
        ],
        out_specs=pl.BlockSpec((1, TS, d), lambda i, j: (i, j, 0)),
        out_shape=jax.ShapeDtypeStruct((b, s, d), F32),
        scratch_shapes=[
            pltpu.VMEM((HA_HEADS, HA_DV, HA_DK), F32),
            pltpu.VMEM((TS, hk), BF16), pltpu.VMEM((TS, hk), BF16), pltpu.VMEM((TS, hk), BF16),
            pltpu.VMEM((TS, hk), BF16), pltpu.VMEM((TS, hk), F32), pltpu.VMEM((TS, hk), BF16),
        ],
        compiler_params=pltpu.CompilerParams(
            dimension_semantics=("arbitrary", "arbitrary"), vmem_limit_bytes=VMEM_LIMIT),
        name="hgrn",
    )(x, nw, wa, wga, lb_raw, gnw, wpa, tri)


def _lru_kernel(x_ref, ya_ref, nw_ref, wb_ref, cw_ref, cb_ref, wra_ref, bra_ref, wrx_ref, brx_ref, lam_ref,
                wpb_ref, wout_ref, nffn_ref, wrt_ref, brt_ref, ustr_ref,
                x1_ref, h2_ref, ri_ref, rw_ref, cnt_ref,
                xbuf_ref, a_ref, u_ref, hl_ref, pp_ref, hc_ref, carry_ref):
    seg = TS // SUBLANES
    w = LRU_WIDTH
    bi, si = pl.program_id(0), pl.program_id(1)

    @pl.when(si == 0)
    def _():
        xbuf_ref[0:SUBLANES, :] = jnp.zeros((SUBLANES, w), F32)
        hc_ref[...] = jnp.zeros_like(hc_ref)

    @pl.when((bi == 0) & (si == 0))
    def _():
        carry_ref[...] = jnp.zeros_like(carry_ref)

    x = x_ref[0]
    hb = _rms(x, nw_ref[...]).astype(BF16)

    xb = jnp.dot(hb, wb_ref[:, 0:w], preferred_element_type=F32)
    xbuf_ref[SUBLANES:SUBLANES + TS, :] = xb
    cw = cw_ref[...]
    xc = cb_ref[...] + cw[CONV_WIDTH - 1:CONV_WIDTH, :] * xb
    for kk in range(CONV_WIDTH - 1):
        shift = CONV_WIDTH - 1 - kk
        xc = xc + cw[kk:kk + 1, :] * xbuf_ref[SUBLANES - shift:SUBLANES - shift + TS, :]
    xbuf_ref[0:SUBLANES, :] = xbuf_ref[TS:TS + SUBLANES, :]

    xcb = xc.astype(BF16)
    r_parts, i_parts = [], []
    for j in range(LRU_HEADS):
        l = slice(j * LRU_BLOCK, (j + 1) * LRU_BLOCK)
        r_parts.append(jnp.dot(xcb[:, l], wra_ref[j], preferred_element_type=F32))
        i_parts.append(jnp.dot(xcb[:, l], wrx_ref[j], preferred_element_type=F32))
    rg = jax.nn.sigmoid(jnp.concatenate(r_parts, axis=1) + bra_ref[...])
    ig = jax.nn.sigmoid(jnp.concatenate(i_parts, axis=1) + brx_ref[...])
    log_a = RG_C * rg * jax.nn.log_sigmoid(lam_ref[...])
    a = jnp.exp(log_a)
    mult = jnp.sqrt(1.0 - a * a)
    is_first = (lax.broadcasted_iota(jnp.int32, (TS, 1), 0) == 0) & (si == 0)
    mult = jnp.where(is_first, 1.0, mult)
    u = xc * ig * mult
    n_groups = w // LANES
    for gi in range(n_groups):
        a_ref[gi * TS:(gi + 1) * TS, :] = a[:, gi * LANES:(gi + 1) * LANES]
        u_ref[gi * TS:(gi + 1) * TS, :] = u[:, gi * LANES:(gi + 1) * LANES]

    def step(i, hp):
        hs, ps = hp
        new_h, new_p = [], []
        for gi in range(n_groups):
            rows = pl.ds(gi * TS + i, SUBLANES, stride=seg)
            ai = a_ref[rows, :]
            h = ai * hs[gi] + u_ref[rows, :]
            p = ai * ps[gi]
            hl_ref[rows, :] = h
            pp_ref[rows, :] = p
            new_h.append(h)
            new_p.append(p)
        return tuple(new_h), tuple(new_p)

    init = (tuple(jnp.zeros((SUBLANES, LANES), F32) for _ in range(n_groups)),
            tuple(jnp.ones((SUBLANES, LANES), F32) for _ in range(n_groups)))
    h_end, p_end = lax.fori_loop(0, seg, step, init, unroll=4)
    h_end = jnp.concatenate(h_end, axis=1)
    p_end = jnp.concatenate(p_end, axis=1)
    c = hc_ref[...]
    h_parts = []
    for j in range(SUBLANES):
        hl = jnp.concatenate([hl_ref[gi * TS + j * seg:gi * TS + (j + 1) * seg, :] for gi in range(n_groups)],
                             axis=1)
        pp = jnp.concatenate([pp_ref[gi * TS + j * seg:gi * TS + (j + 1) * seg, :] for gi in range(n_groups)],
                             axis=1)
        h_parts.append(hl + pp * c)
        c = h_end[j:j + 1, :] + p_end[j:j + 1, :] * c
    hc_ref[...] = c
    h_full = jnp.concatenate(h_parts, axis=0)

    gate_b = jnp.dot(hb, wb_ref[:, w:2 * w], preferred_element_type=F32)
    ob = (h_full * jax.nn.gelu(gate_b)).astype(BF16)
    yb = jnp.dot(ob, wpb_ref[...], preferred_element_type=F32)
    gb = jnp.dot(hb, wb_ref[:, 2 * w:3 * w], preferred_element_type=F32)
    merged = ya_ref[0] + jax.nn.sigmoid(gb) * yb
    x1 = x + jnp.dot(merged.astype(BF16), wout_ref[...], preferred_element_type=F32)
    x1_ref[0] = x1

    h2 = _rms(x1, nffn_ref[...])
    h_hi = h2.astype(BF16)
    h2_ref[0] = h_hi
    h_lo = (h2 - h_hi.astype(F32)).astype(BF16)
    wt = wrt_ref[...]
    w_hi = wt.astype(BF16)
    w_lo = (wt - w_hi.astype(F32)).astype(BF16)
    logits = (lax.dot_general(w_hi, h_hi, NT_DIMS, preferred_element_type=F32)
              + lax.dot_general(w_hi, h_lo, NT_DIMS, preferred_element_type=F32)
              + lax.dot_general(w_lo, h_hi, NT_DIMS, preferred_element_type=F32)) + brt_ref[...]

    eio = lax.broadcasted_iota(jnp.int32, (N_EXPERTS, TS), 0)
    idxs, vals = [], []
    for _ in range(TOP_K):
        m = jnp.max(logits, axis=0, keepdims=True)
        ik = jnp.min(jnp.where(logits == m, eio, N_EXPERTS), axis=0, keepdims=True)
        idxs.append(ik)
        vals.append(m)
        logits = jnp.where(eio == ik, -jnp.inf, logits)
    exps = [jnp.exp(v - vals[0]) for v in vals]
    den = functools.reduce(lambda p, q: p + q, exps)
    onehots = [(eio == ik) for ik in idxs]
    oh = functools.reduce(lambda p, q: p + q, [o.astype(F32) for o in onehots])
    cum = jnp.dot(oh.astype(BF16), ustr_ref[...], preferred_element_type=F32) + carry_ref[:, 0:1]
    poss = [jnp.sum(jnp.where(o, cum, 0.0), axis=0, keepdims=True).astype(jnp.int32) for o in onehots]
    carry_ref[...] = carry_ref[...] + jnp.sum(oh, axis=1, keepdims=True)
    ri_ref[0] = jnp.concatenate(idxs + poss, axis=0)
    rw_ref[0] = jnp.concatenate([e / den for e in exps] + [jnp.zeros((SUBLANES - TOP_K, TS), F32)], axis=0)
    cnt_ref[...] = carry_ref[...]


def _lru_call(x, ya, nw, wb, cw, cb, wra, bra, wrx, brx, lam, wpb, wout, nffn, wrt, brt, ustr):
    b, s, d = x.shape
    w = LRU_WIDTH
    nt = b * (s // TS)
    const = lambda shape: pl.BlockSpec(shape, lambda i, j: (0,) * len(shape))
    tok = pl.BlockSpec((1, TS, d), lambda i, j: (i, j, 0))
    rt = pl.BlockSpec((1, SUBLANES, TS), lambda i, j: (i * (s // TS) + j, 0, 0))
    return pl.pallas_call(
        _lru_kernel,
        grid=(b, s // TS),
        in_specs=[
            tok, tok, const((1, d)), const((d, 3 * w)), const((CONV_WIDTH, w)), const((1, w)),
            const((LRU_HEADS, LRU_BLOCK, LRU_BLOCK)), const((1, w)),
            const((LRU_HEADS, LRU_BLOCK, LRU_BLOCK)), const((1, w)), const((1, w)),
            const((w, d)), const((d, d)), const((1, d)), const((N_EXPERTS, d)), const((N_EXPERTS, 1)),
            const((TS, TS)),
        ],
        out_specs=[tok, tok, rt, rt, const((N_EXPERTS, LANES))],
        out_shape=[
            jax.ShapeDtypeStruct((b, s, d), F32),
            jax.ShapeDtypeStruct((b, s, d), BF16),
            jax.ShapeDtypeStruct((nt, SUBLANES, TS), jnp.int32),
            jax.ShapeDtypeStruct((nt, SUBLANES, TS), F32),
            jax.ShapeDtypeStruct((N_EXPERTS, LANES), F32),
        ],
        scratch_shapes=[
            pltpu.VMEM((TS + SUBLANES, w), F32),
            pltpu.VMEM((TS * w // LANES, LANES), F32), pltpu.VMEM((TS * w // LANES, LANES), F32),
            pltpu.VMEM((TS * w // LANES, LANES), F32), pltpu.VMEM((TS * w // LANES, LANES), F32),
            pltpu.VMEM((1, w), F32), pltpu.VMEM((N_EXPERTS, LANES), F32),
        ],
        compiler_params=pltpu.CompilerParams(
            dimension_semantics=("arbitrary", "arbitrary"), vmem_limit_bytes=VMEM_LIMIT),
        name="lru_router",
    )(x, ya, nw, wb, cw, cb, wra, bra, wrx, brx, lam, wpb, wout, nffn, wrt, brt, ustr)


def _moe_kernel(te_ref, nu_ref, xs_ref, rw_ref, wg_ref, wl_ref, bg_ref, bl_ref, wd_ref, bd_ref, y_ref):
    i = pl.program_id(0)

    @pl.when(i < nu_ref[0])
    def _():
        xs = xs_ref[...]
        g = jnp.dot(xs, wg_ref[0], preferred_element_type=F32) + bg_ref[0]
        l = jnp.dot(xs, wl_ref[0], preferred_element_type=F32) + bl_ref[0]
        g = jnp.minimum(g, SWIGLU_LIMIT)
        l = jnp.clip(l, -SWIGLU_LIMIT, SWIGLU_LIMIT)
        act = g * jax.nn.sigmoid(SWIGLU_ALPHA * g) * (l + 1.0)
        y = jnp.dot(act.astype(BF16), wd_ref[0], preferred_element_type=F32) + bd_ref[0]
        y_ref[...] = y * rw_ref[...]

    @pl.when(i >= nu_ref[0])
    def _():
        y_ref[...] = jnp.zeros_like(y_ref)


def _moe_call(tile_expert, n_used, xs, roww, wg, wl, bg, bl, wd, bd):
    p, d = xs.shape
    n_tiles = p // TM
    row = lambda i, te, nu: (jnp.minimum(i, nu[0] - 1), 0)
    ex3 = lambda i, te, nu: (te[i], 0, 0)
    grid_spec = pltpu.PrefetchScalarGridSpec(
        num_scalar_prefetch=2,
        grid=(n_tiles,),
        in_specs=[
            pl.BlockSpec((TM, d), row), pl.BlockSpec((TM, 1), row),
            pl.BlockSpec((1, d, D_FF), ex3), pl.BlockSpec((1, d, D_FF), ex3),
            pl.BlockSpec((1, 1, D_FF), ex3), pl.BlockSpec((1, 1, D_FF), ex3),
            pl.BlockSpec((1, D_FF, d), ex3), pl.BlockSpec((1, 1, d), ex3),
        ],
        out_specs=pl.BlockSpec((TM, d), lambda i, te, nu: (i, 0)),
    )
    return pl.pallas_call(
        _moe_kernel,
        grid_spec=grid_spec,
        out_shape=jax.ShapeDtypeStruct((p, d), F32),
        compiler_params=pltpu.CompilerParams(
            dimension_semantics=("arbitrary",), vmem_limit_bytes=VMEM_LIMIT),
        name="moe_experts",
    )(tile_expert, n_used, xs, roww, wg, wl, bg, bl, wd, bd)


def _final_kernel(x1_ref, moe_ref, nw_ref, out_ref):
    out_ref[...] = _rms(x1_ref[...] + moe_ref[...], nw_ref[...])


def _final_call(x1, moe, nw):
    t, d = x1.shape
    blk = pl.BlockSpec((TF, d), lambda i: (i, 0))
    return pl.pallas_call(
        _final_kernel,
        grid=(t // TF,),
        in_specs=[blk, blk, pl.BlockSpec((1, d), lambda i: (0, 0))],
        out_specs=blk,
        out_shape=jax.ShapeDtypeStruct((t, d), F32),
        compiler_params=pltpu.CompilerParams(dimension_semantics=("arbitrary",)),
        name="final_norm",
    )(x1, moe, nw)


def _chunk_masks():
    r = lax.broadcasted_iota(jnp.int32, (TS, TS), 0)
    c = lax.broadcasted_iota(jnp.int32, (TS, TS), 1)
    tri = ((r // CHUNK == c // CHUNK) & (r >= c)).astype(BF16)
    ustr = (r < c).astype(BF16)
    return tri, ustr


def kernel(x, norm_mix, w_in, lb_raw, gn_w, conv_w, conv_b, w_rg_a, b_rg_a, w_rg_x, b_rg_x, lru_lambda,
           w_proj_a, w_proj_b, w_out, norm_ffn, w_router, b_router, w_gu, b_gu, w_down, b_down, norm_final):
    b, s, d = x.shape
    t = b * s
    assert w_in.shape[0] == 1 and s % TS == 0 and t % TF == 0
    hk = HA_HEADS * HA_DK
    w = LRU_WIDTH
    tri, ustr = _chunk_masks()

    wi = w_in[0]
    o_lru = 4 * hk
    wa = wi[:, 0:o_lru].astype(BF16)
    wb = jnp.concatenate([wi[:, o_lru:o_lru + 2 * w], wi[:, o_lru + 2 * w + d:o_lru + 2 * w + 2 * d]],
                         axis=1).astype(BF16)
    wga = wi[:, o_lru + 2 * w:o_lru + 2 * w + d].astype(BF16)
    row = lambda v: v.reshape(1, -1)

    ya = _hgrn_call(x, row(norm_mix[0]), wa, wga, lb_raw, row(gn_w[0]), w_proj_a[0].astype(BF16), tri)
    x1, h2, ri, rw, cnt = _lru_call(
        x, ya, row(norm_mix[0]), wb, conv_w[0], row(conv_b[0]), w_rg_a[0].astype(BF16), row(b_rg_a[0]),
        w_rg_x[0].astype(BF16), row(b_rg_x[0]), row(lru_lambda[0]), w_proj_b[0].astype(BF16),
        w_out[0].astype(BF16), row(norm_ffn[0]), w_router[0].T, b_router[0].reshape(-1, 1), ustr)

    ri = ri.transpose(1, 0, 2).reshape(SUBLANES, t)
    idx, pos = ri[0:TOP_K], ri[TOP_K:2 * TOP_K]
    gate_w = rw.transpose(1, 0, 2).reshape(SUBLANES, t)[0:TOP_K]
    counts = cnt[:, 0].astype(jnp.int32)
    padded = ((counts + TM - 1) // TM) * TM
    ends = jnp.cumsum(padded)
    offs = ends - padded
    slot = jnp.take(offs, idx) + pos
    n_rows = TOP_K * t + N_EXPERTS * TM
    n_tiles = n_rows // TM
    tile_expert = jnp.minimum(
        jnp.sum(jnp.arange(n_tiles, dtype=jnp.int32)[:, None] * TM >= ends[None, :], axis=1),
        N_EXPERTS - 1).astype(jnp.int32)
    n_used = (ends[-1:] // TM).astype(jnp.int32)

    flat_slot = slot.reshape(-1)
    tok_of_slot = jnp.zeros((n_rows,), jnp.int32).at[flat_slot].set(
        jnp.tile(jnp.arange(t, dtype=jnp.int32), TOP_K), unique_indices=True)
    roww = jnp.zeros((n_rows,), F32).at[flat_slot].set(gate_w.reshape(-1), unique_indices=True)
    xs = jnp.take(h2.reshape(t, d), tok_of_slot, axis=0)

    wgu = w_gu[0]
    y = _moe_call(
        tile_expert, n_used, xs, roww.reshape(n_rows, 1),
        wgu[:, :, 0::2].astype(BF16), wgu[:, :, 1::2].astype(BF16),
        b_gu[0][:, 0::2].reshape(N_EXPERTS, 1, D_FF), b_gu[0][:, 1::2].reshape(N_EXPERTS, 1, D_FF),
        w_down[0].astype(BF16), b_down[0].reshape(N_EXPERTS, 1, d))

    moe = jnp.sum(jnp.take(y, slot, axis=0), axis=0)
    out = _final_call(x1.reshape(t, d), moe, row(norm_final))
    return out.reshape(b, s, d)
```

```python
import functools

import jax
import jax.numpy as jnp
from jax import lax
from jax.experimental import pallas as pl
from jax.experimental.pallas import tpu as pltpu

F32 = jnp.float32
BF16 = jnp.bfloat16

D_MODEL = 1024
HA_HEADS = 8
HA_DK = 128
HA_DV = 128
CHUNK = 64
LRU_WIDTH = 1024
LRU_HEADS = 4
LRU_BLOCK = LRU_WIDTH // LRU_HEADS
CONV_WIDTH = 4
RG_C = 8.0
N_EXPERTS = 32
TOP_K = 4
D_FF = 1024
SWIGLU_LIMIT = 7.0
SWIGLU_ALPHA = 1.702
EPS = 1e-5

SUBLANES = 8
LANES = 128
MXU_COLS = 256
TS = 256
TM = 512
TF = 512
VMEM_LIMIT = 56 * 1024 * 1024

NT_DIMS = (((1,), (1,)), ((), ()))
TN_DIMS = (((0,), (0,)), ((), ()))


def _rms(x, w):
    return x * lax.rsqrt(jnp.mean(x * x, axis=-1, keepdims=True) + EPS) * w


def _split3(v):
    hi = v.astype(BF16)
    r1 = v - hi.astype(F32)
    mid = r1.astype(BF16)
    lo = (r1 - mid.astype(F32)).astype(BF16)
    return hi, mid, lo


def _hgrn_kernel(x_ref, nw_ref, wa_ref, wga_ref, lbraw_ref, gnw_ref, wpa_ref, tri_ref, out_ref,
                 st_ref, qin_ref, kin_ref, kdec_ref, v_ref, g_ref, oa_ref):
    n_chunks = TS // CHUNK
    hk = HA_HEADS * HA_DK

    @pl.when(pl.program_id(1) == 0)
    def _():
        st_ref[...] = jnp.zeros_like(st_ref)

    hb = _rms(x_ref[0], nw_ref[...]).astype(BF16)

    lbr = lbraw_ref[...]
    rows = [lbr[j:j + 1, :] for j in range(lbr.shape[0])]
    mx = functools.reduce(jnp.maximum, rows)
    es = [jnp.exp(r - mx) for r in rows]
    lb = es[0] / functools.reduce(lambda a, b: a + b, es)

    fl = jnp.dot(hb, wa_ref[:, hk:2 * hk], preferred_element_type=F32)
    f = lb + (1.0 - lb) * jax.nn.sigmoid(fl)
    lf = jnp.log(f)
    k = 1.0 - f
    tri = tri_ref[...]
    bcum = functools.reduce(lambda a, b: a + b,
                            [jnp.dot(tri, p, preferred_element_type=F32) for p in _split3(lf)])
    b_last = [bcum[c * CHUNK + CHUNK - 1:c * CHUNK + CHUNK, :] for c in range(n_chunks)]
    bl_full = jnp.concatenate([jnp.broadcast_to(b, (CHUNK, hk)) for b in b_last], axis=0)

    q = jnp.dot(hb, wa_ref[:, 0:hk], preferred_element_type=F32)
    qin_ref[...] = (jax.nn.silu(q) * jnp.exp(bcum)).astype(BF16)
    kin_ref[...] = (k * jnp.exp(-bcum)).astype(BF16)
    kdec_ref[...] = (k * jnp.exp(bl_full - bcum)).astype(BF16)
    v_ref[...] = jnp.dot(hb, wa_ref[:, 2 * hk:3 * hk], preferred_element_type=F32).astype(BF16)
    g = jnp.dot(hb, wa_ref[:, 3 * hk:4 * hk], preferred_element_type=F32)
    g_ref[...] = jax.nn.silu(g) * gnw_ref[...]

    causal = (lax.broadcasted_iota(jnp.int32, (CHUNK, CHUNK), 0)
              >= lax.broadcasted_iota(jnp.int32, (CHUNK, CHUNK), 1))
    for c in range(n_chunks):
        r = slice(c * CHUNK, (c + 1) * CHUNK)
        edec = jnp.exp(b_last[c])
        for h in range(HA_HEADS):
            l = slice(h * HA_DK, (h + 1) * HA_DK)
            qh, kh, kd, vh = qin_ref[r, l], kin_ref[r, l], kdec_ref[r, l], v_ref[r, l]
            s = lax.dot_general(qh, kh, NT_DIMS, preferred_element_type=F32)
            s = jnp.where(causal, s, 0.0).astype(BF16)
            st = st_ref[h]
            o = (jnp.dot(s, vh, preferred_element_type=F32)
                 + lax.dot_general(qh, st.astype(BF16), NT_DIMS, preferred_element_type=F32))
            st_ref[h] = st * edec[:, l] + lax.dot_general(vh, kd, TN_DIMS, preferred_element_type=F32)
            ms = jnp.mean(o * o, axis=-1, keepdims=True)
            oa_ref[r, l] = (o * lax.rsqrt(ms + EPS) * g_ref[r, l]).astype(BF16)

    ya = jnp.dot(oa_ref[...], wpa_ref[...], preferred_element_type=F32)
    ga = jnp.dot(hb, wga_ref[...], preferred_element_type=F32)
    out_ref[0] = jax.nn.sigmoid(ga) * ya


def _hgrn_call(x, nw, wa, wga, lb_raw, gnw, wpa, tri):
    b, s, d = x.shape
    hk = HA_HEADS * HA_DK
    const = lambda shape: pl.BlockSpec(shape, lambda i, j: (0,) * len(shape))
    return pl.pallas_call(
        _hgrn_kernel,
        grid=(b, s // TS),
        in_specs=[
            pl.BlockSpec((1, TS, d), lambda i, j: (i, j, 0)),
            const((1, d)), const((d, 4 * hk)), const((d, d)), const(lb_raw.shape), const((1, hk)),
            const((hk, d)), const((TS, TS)),
        ],
        out_specs=pl.BlockSpec((1, TS, d), lambda i, j: (i, j, 0)),
        out_shape=jax.ShapeDtypeStruct((b, s, d), F32),
        scratch_shapes=[
            pltpu.VMEM((HA_HEADS, HA_DV, HA_DK), F32),
            pltpu.VMEM((TS, hk), BF16), pltpu.VMEM((TS, hk), BF16), pltpu.VMEM((TS, hk), BF16),
            pltpu.VMEM((TS, hk), BF16), pltpu.VMEM((TS, hk), F32), pltpu.VMEM((TS, hk), BF16),
        ],
        compiler_params=pltpu.CompilerParams(
            dimension_semantics=("arbitrary", "arbitrary"), vmem_limit_bytes=VMEM_LIMIT),
        name="hgrn",
    )(x, nw, wa, wga, lb_raw, gnw, wpa, tri)


def _lru_kernel(x_ref, ya_ref, nw_ref, wb_ref, cw_ref, cb_ref, wra_ref, bra_ref, wrx_ref, brx_ref, lam_ref,
                wpb_ref, wout_ref, nffn_ref, wrt_ref, brt_ref, ustr_ref,
                x1_ref, h2_ref, ri_ref, rw_ref, cnt_ref,
                xbuf_ref, a_ref, u_ref, hl_ref, pp_ref, hc_ref, carry_ref):
    seg = TS // SUBLANES
    w = LRU_WIDTH
    bi, si = pl.program_id(0), pl.program_id(1)

    @pl.when(si == 0)
    def _():
        xbuf_ref[0:SUBLANES, :] = jnp.zeros((SUBLANES, w), F32)
        hc_ref[...] = jnp.zeros_like(hc_ref)

    @pl.when((bi == 0) & (si == 0))
    def _():
        carry_ref[...] = jnp.zeros_like(carry_ref)

    x = x_ref[0]
    hb = _rms(x, nw_ref[...]).astype(BF16)

    xb = jnp.dot(hb, wb_ref[:, 0:w], preferred_element_type=F32)
    xbuf_ref[SUBLANES:SUBLANES + TS, :] = xb
    cw = cw_ref[...]
    xc = cb_ref[...] + cw[CONV_WIDTH - 1:CONV_WIDTH, :] * xb
    for kk in range(CONV_WIDTH - 1):
        shift = CONV_WIDTH - 1 - kk
        xc = xc + cw[kk:kk + 1, :] * xbuf_ref[SUBLANES - shift:SUBLANES - shift + TS, :]
    xbuf_ref[0:SUBLANES, :] = xbuf_ref[TS:TS + SUBLANES, :]

    xcb = xc.astype(BF16)
    r_parts, i_parts = [], []
    for j in range(LRU_HEADS):
        l = slice(j * LRU_BLOCK, (j + 1) * LRU_BLOCK)
        r_parts.append(jnp.dot(xcb[:, l], wra_ref[j], preferred_element_type=F32))
        i_parts.append(jnp.dot(xcb[:, l], wrx_ref[j], preferred_element_type=F32))
    rg = jax.nn.sigmoid(jnp.concatenate(r_parts, axis=1) + bra_ref[...])
    ig = jax.nn.sigmoid(jnp.concatenate(i_parts, axis=1) + brx_ref[...])
    log_a = RG_C * rg * jax.nn.log_sigmoid(lam_ref[...])
    a = jnp.exp(log_a)
    mult = jnp.sqrt(1.0 - a * a)
    is_first = (lax.broadcasted_iota(jnp.int32, (TS, 1), 0) == 0) & (si == 0)
    mult = jnp.where(is_first, 1.0, mult)
    u = xc * ig * mult
    n_groups = w // LANES
    for gi in range(n_groups):
        a_ref[gi * TS:(gi + 1) * TS, :] = a[:, gi * LANES:(gi + 1) * LANES]
        u_ref[gi * TS:(gi + 1) * TS, :] = u[:, gi * LANES:(gi + 1) * LANES]

    def step(i, hp):
        hs, ps = hp
        new_h, new_p = [], []
        for gi in range(n_groups):
            rows = pl.ds(gi * TS + i, SUBLANES, stride=seg)
            ai = a_ref[rows, :]
            h = ai * hs[gi] + u_ref[rows, :]
            p = ai * ps[gi]
            hl_ref[rows, :] = h
            pp_ref[rows, :] = p
            new_h.append(h)
            new_p.append(p)
        return tuple(new_h), tuple(new_p)

    init = (tuple(jnp.zeros((SUBLANES, LANES), F32) for _ in range(n_groups)),
            tuple(jnp.ones((SUBLANES, LANES), F32) for _ in range(n_groups)))
    h_end, p_end = lax.fori_loop(0, seg, step, init, unroll=4)
    h_end = jnp.concatenate(h_end, axis=1)
    p_end = jnp.concatenate(p_end, axis=1)
    c = hc_ref[...]
    h_parts = []
    for j in range(SUBLANES):
        hl = jnp.concatenate([hl_ref[gi * TS + j * seg:gi * TS + (j + 1) * seg, :] for gi in range(n_groups)],
                             axis=1)
        pp = jnp.concatenate([pp_ref[gi * TS + j * seg:gi * TS + (j + 1) * seg, :] for gi in range(n_groups)],
                             axis=1)
        h_parts.append(hl + pp * c)
        c = h_end[j:j + 1, :] + p_end[j:j + 1, :] * c
    hc_ref[...] = c
    h_full = jnp.concatenate(h_parts, axis=0)

    gate_b = jnp.dot(hb, wb_ref[:, w:2 * w], preferred_element_type=F32)
    ob = (h_full * jax.nn.gelu(gate_b)).astype(BF16)
    yb = jnp.dot(ob, wpb_ref[...], preferred_element_type=F32)
    gb = jnp.dot(hb, wb_ref[:, 2 * w:3 * w], preferred_element_type=F32)
    merged = ya_ref[0] + jax.nn.sigmoid(gb) * yb
    x1 = x + jnp.dot(merged.astype(BF16), wout_ref[...], preferred_element_type=F32)
    x1_ref[0] = x1

    h2 = _rms(x1, nffn_ref[...])
    h_hi = h2.astype(BF16)
    h2_ref[0] = h_hi
    h_lo = (h2 - h_hi.astype(F32)).astype(BF16)
    wt = wrt_ref[...]
    w_hi = wt.astype(BF16)
    w_lo = (wt - w_hi.astype(F32)).astype(BF16)
    logits = (lax.dot_general(w_hi, h_hi, NT_DIMS, preferred_element_type=F32)
              + lax.dot_general(w_hi, h_lo, NT_DIMS, preferred_element_type=F32)
              + lax.dot_general(w_lo, h_hi, NT_DIMS, preferred_element_type=F32)) + brt_ref[...]

    eio = lax.broadcasted_iota(jnp.int32, (N_EXPERTS, TS), 0)
    idxs, vals = [], []
    for _ in range(TOP_K):
        m = jnp.max(logits, axis=0, keepdims=True)
        ik = jnp.min(jnp.where(logits == m, eio, N_EXPERTS), axis=0, keepdims=True)
        idxs.append(ik)
        vals.append(m)
        logits = jnp.where(eio == ik, -jnp.inf, logits)
    exps = [jnp.exp(v - vals[0]) for v in vals]
    den = functools.reduce(lambda p, q: p + q, exps)
    onehots = [(eio == ik) for ik in idxs]
    oh = functools.reduce(lambda p, q: p + q, [o.astype(F32) for o in onehots])
    cum = jnp.dot(oh.astype(BF16), ustr_ref[...], preferred_element_type=F32) + carry_ref[:, 0:1]
    poss = [jnp.sum(jnp.where(o, cum, 0.0), axis=0, keepdims=True).astype(jnp.int32) for o in onehots]
    carry_ref[...] = carry_ref[...] + jnp.sum(oh, axis=1, keepdims=True)
    ri_ref[0] = jnp.concatenate(idxs + poss, axis=0)
    rw_ref[0] = jnp.concatenate([e / den for e in exps] + [jnp.zeros((SUBLANES - TOP_K, TS), F32)], axis=0)
    cnt_ref[...] = carry_ref[...]


def _lru_call(x, ya, nw, wb, cw, cb, wra, bra, wrx, brx, lam, wpb, wout, nffn, wrt, brt, ustr):
    b, s, d = x.shape
    w = LRU_WIDTH
    nt = b * (s // TS)
    const = lambda shape: pl.BlockSpec(shape, lambda i, j: (0,) * len(shape))
    tok = pl.BlockSpec((1, TS, d), lambda i, j: (i, j, 0))
    rt = pl.BlockSpec((1, SUBLANES, TS), lambda i, j: (i * (s // TS) + j, 0, 0))
    return pl.pallas_call(
        _lru_kernel,
        grid=(b, s // TS),
        in_specs=[
            tok, tok, const((1, d)), const((d, 3 * w)), const((CONV_WIDTH, w)), const((1, w)),
            const((LRU_HEADS, LRU_BLOCK, LRU_BLOCK)), const((1, w)),
            const((LRU_HEADS, LRU_BLOCK, LRU_BLOCK)), const((1, w)), const((1, w)),
            const((w, d)), const((d, d)), const((1, d)), const((N_EXPERTS, d)), const((N_EXPERTS, 1)),
            const((TS, TS)),
        ],
        out_specs=[tok, tok, rt, rt, const((N_EXPERTS, LANES))],
        out_shape=[
            jax.ShapeDtypeStruct((b, s, d), F32),
            jax.ShapeDtypeStruct((b, s, d), BF16),
            jax.ShapeDtypeStruct((nt, SUBLANES, TS), jnp.int32),
            jax.ShapeDtypeStruct((nt, SUBLANES, TS), F32),
            jax.ShapeDtypeStruct((N_EXPERTS, LANES), F32),
        ],
        scratch_shapes=[
            pltpu.VMEM((TS + SUBLANES, w), F32),
            pltpu.VMEM((TS * w // LANES, LANES), F32), pltpu.VMEM((TS * w // LANES, LANES), F32),
            pltpu.VMEM((TS * w // LANES, LANES), F32), pltpu.VMEM((TS * w // LANES, LANES), F32),
            pltpu.VMEM((1, w), F32), pltpu.VMEM((N_EXPERTS, LANES), F32),
        ],
        compiler_params=pltpu.CompilerParams(
            dimension_semantics=("arbitrary", "arbitrary"), vmem_limit_bytes=VMEM_LIMIT),
        name="lru_router",
    )(x, ya, nw, wb, cw, cb, wra, bra, wrx, brx, lam, wpb, wout, nffn, wrt, brt, ustr)


def _moe_kernel(te_ref, nu_ref, xs_ref, rw_ref, wgu_ref, bg_ref, bl_ref, wd_ref, bd_ref, perm_ref, y_ref,
                wg_s, wl_s, wd_s):
    i = pl.program_id(0)
    half = MXU_COLS // 2
    new_expert = (i == 0) | (te_ref[i] != te_ref[jnp.maximum(i - 1, 0)])

    @pl.when(new_expert & (i < nu_ref[0]))
    def _():
        for blk in range(2 * D_FF // MXU_COLS):
            wblk = wgu_ref[0, :, blk * MXU_COLS:(blk + 1) * MXU_COLS].astype(BF16)
            pw = jnp.dot(wblk, perm_ref[...], preferred_element_type=F32)
            wg_s[:, blk * half:(blk + 1) * half] = pw[:, :half].astype(BF16)
            wl_s[:, blk * half:(blk + 1) * half] = pw[:, half:].astype(BF16)
        wd_s[...] = wd_ref[0].astype(BF16)

    @pl.when(i < nu_ref[0])
    def _():
        xs = xs_ref[...]
        g = jnp.dot(xs, wg_s[...], preferred_element_type=F32) + bg_ref[0]
        l = jnp.dot(xs, wl_s[...], preferred_element_type=F32) + bl_ref[0]
        g = jnp.minimum(g, SWIGLU_LIMIT)
        l = jnp.clip(l, -SWIGLU_LIMIT, SWIGLU_LIMIT)
        act = g * jax.nn.sigmoid(SWIGLU_ALPHA * g) * (l + 1.0)
        y = jnp.dot(act.astype(BF16), wd_s[...], preferred_element_type=F32) + bd_ref[0]
        y_ref[...] = y * rw_ref[...]

    @pl.when(i >= nu_ref[0])
    def _():
        y_ref[...] = jnp.zeros_like(y_ref)


def _moe_call(tile_expert, n_used, xs, roww, wgu, bg, bl, wd, bd):
    p, d = xs.shape
    n_tiles = p // TM
    half = MXU_COLS // 2
    r = lax.broadcasted_iota(jnp.int32, (MXU_COLS, MXU_COLS), 0)
    c = lax.broadcasted_iota(jnp.int32, (MXU_COLS, MXU_COLS), 1)
    perm = (r == jnp.where(c < half, 2 * c, 2 * (c - half) + 1)).astype(BF16)
    row = lambda i, te, nu: (jnp.minimum(i, nu[0] - 1), 0)
    ex3 = lambda i, te, nu: (te[i], 0, 0)
    grid_spec = pltpu.PrefetchScalarGridSpec(
        num_scalar_prefetch=2,
        grid=(n_tiles,),
        in_specs=[
            pl.BlockSpec((TM, d), row), pl.BlockSpec((TM, 1), row),
            pl.BlockSpec((1, d, 2 * D_FF), ex3),
            pl.BlockSpec((1, 1, D_FF), ex3), pl.BlockSpec((1, 1, D_FF), ex3),
            pl.BlockSpec((1, D_FF, d), ex3), pl.BlockSpec((1, 1, d), ex3),
            pl.BlockSpec((MXU_COLS, MXU_COLS), lambda i, te, nu: (0, 0)),
        ],
        out_specs=pl.BlockSpec((TM, d), lambda i, te, nu: (i, 0)),
        scratch_shapes=[pltpu.VMEM((d, D_FF), BF16), pltpu.VMEM((d, D_FF), BF16), pltpu.VMEM((D_FF, d), BF16)],
    )
    return pl.pallas_call(
        _moe_kernel,
        grid_spec=grid_spec,
        out_shape=jax.ShapeDtypeStruct((p, d), F32),
        compiler_params=pltpu.CompilerParams(
            dimension_semantics=("arbitrary",), vmem_limit_bytes=VMEM_LIMIT),
        name="moe_experts",
    )(tile_expert, n_used, xs, roww, wgu, bg, bl, wd, bd, perm)


def _final_kernel(x1_ref, moe_ref, nw_ref, out_ref):
    out_ref[...] = _rms(x1_ref[...] + moe_ref[...], nw_ref[...])


def _final_call(x1, moe, nw):
    t, d = x1.shape
    blk = pl.BlockSpec((TF, d), lambda i: (i, 0))
    return pl.pallas_call(
        _final_kernel,
        grid=(t // TF,),
        in_specs=[blk, blk, pl.BlockSpec((1, d), lambda i: (0, 0))],
        out_specs=blk,
        out_shape=jax.ShapeDtypeStruct((t, d), F32),
        compiler_params=pltpu.CompilerParams(dimension_semantics=("arbitrary",)),
        name="final_norm",
    )(x1, moe, nw)


def _chunk_masks():
    r = lax.broadcasted_iota(jnp.int32, (TS, TS), 0)
    c = lax.broadcasted_iota(jnp.int32, (TS, TS), 1)
    tri = ((r // CHUNK == c // CHUNK) & (r >= c)).astype(BF16)
    ustr = (r < c).astype(BF16)
    return tri, ustr


def kernel(x, norm_mix, w_in, lb_raw, gn_w, conv_w, conv_b, w_rg_a, b_rg_a, w_rg_x, b_rg_x, lru_lambda,
           w_proj_a, w_proj_b, w_out, norm_ffn, w_router, b_router, w_gu, b_gu, w_down, b_down, norm_final):
    b, s, d = x.shape
    t = b * s
    assert w_in.shape[0] == 1 and s % TS == 0 and t % TF == 0
    hk = HA_HEADS * HA_DK
    w = LRU_WIDTH
    tri, ustr = _chunk_masks()

    wi = w_in[0]
    o_lru = 4 * hk
    wa = wi[:, 0:o_lru].astype(BF16)
    wb = jnp.concatenate([wi[:, o_lru:o_lru + 2 * w], wi[:, o_lru + 2 * w + d:o_lru + 2 * w + 2 * d]],
                         axis=1).astype(BF16)
    wga = wi[:, o_lru + 2 * w:o_lru + 2 * w + d].astype(BF16)
    row = lambda v: v.reshape(1, -1)

    ya = _hgrn_call(x, row(norm_mix[0]), wa, wga, lb_raw, row(gn_w[0]), w_proj_a[0].astype(BF16), tri)
    x1, h2, ri, rw, cnt = _lru_call(
        x, ya, row(norm_mix[0]), wb, conv_w[0], row(conv_b[0]), w_rg_a[0].astype(BF16), row(b_rg_a[0]),
        w_rg_x[0].astype(BF16), row(b_rg_x[0]), row(lru_lambda[0]), w_proj_b[0].astype(BF16),
        w_out[0].astype(BF16), row(norm_ffn[0]), w_router[0].T, b_router[0].reshape(-1, 1), ustr)

    ri = ri.transpose(1, 0, 2).reshape(SUBLANES, t)
    idx, pos = ri[0:TOP_K], ri[TOP_K:2 * TOP_K]
    gate_w = rw.transpose(1, 0, 2).reshape(SUBLANES, t)[0:TOP_K]
    counts = cnt[:, 0].astype(jnp.int32)
    padded = ((counts + TM - 1) // TM) * TM
    ends = jnp.cumsum(padded)
    offs = ends - padded
    slot = jnp.take(offs, idx) + pos
    n_rows = TOP_K * t + N_EXPERTS * TM
    n_tiles = n_rows // TM
    tile_expert = jnp.minimum(
        jnp.sum(jnp.arange(n_tiles, dtype=jnp.int32)[:, None] * TM >= ends[None, :], axis=1),
        N_EXPERTS - 1).astype(jnp.int32)
    n_used = (ends[-1:] // TM).astype(jnp.int32)

    asg_of_slot = jnp.zeros((n_rows,), jnp.int32).at[slot.reshape(-1)].set(
        jnp.arange(TOP_K * t, dtype=jnp.int32), unique_indices=True)
    roww = jnp.take(gate_w.reshape(-1), asg_of_slot)
    xs = jnp.take(h2.reshape(t, d), asg_of_slot % t, axis=0)

    y = _moe_call(
        tile_expert, n_used, xs, roww.reshape(n_rows, 1), w_gu[0],
        b_gu[0][:, 0::2].reshape(N_EXPERTS, 1, D_FF), b_gu[0][:, 1::2].reshape(N_EXPERTS, 1, D_FF),
        w_down[0], b_down[0].reshape(N_EXPERTS, 1, d))

    moe = jnp.sum(jnp.take(y, slot, axis=0), axis=0)
    out = _final_call(x1.reshape(t, d), moe, row(norm_final))
    return out.reshape(b, s, d)
```

```python
import functools

import jax
import jax.numpy as jnp
from jax import lax
from jax.experimental import pallas as pl
from jax.experimental.pallas import tpu as pltpu

F32 = jnp.float32
BF16 = jnp.bfloat16

D_MODEL = 1024
HA_HEADS = 8
HA_DK = 128
HA_DV = 128
CHUNK = 64
LRU_WIDTH = 1024
LRU_HEADS = 4
LRU_BLOCK = LRU_WIDTH // LRU_HEADS
CONV_WIDTH = 4
RG_C = 8.0
N_EXPERTS = 32
TOP_K = 4
D_FF = 1024
SWIGLU_LIMIT = 7.0
SWIGLU_ALPHA = 1.702
EPS = 1e-5

SUBLANES = 8
LANES = 128
MXU_COLS = 256
TSH = 256
TSL = 512
TM = 512
TF = 512
VMEM_LIMIT = 56 * 1024 * 1024

NT_DIMS = (((1,), (1,)), ((), ()))
TN_DIMS = (((0,), (0,)), ((), ()))


def _rms(x, w):
    return x * lax.rsqrt(jnp.mean(x * x, axis=-1, keepdims=True) + EPS) * w


def _split3(v):
    hi = v.astype(BF16)
    r1 = v - hi.astype(F32)
    mid = r1.astype(BF16)
    lo = (r1 - mid.astype(F32)).astype(BF16)
    return hi, mid, lo


def _hgrn_kernel(x_ref, nw_ref, wa_ref, wga_ref, lbraw_ref, gnw_ref, wpa_ref, tri_ref, out_ref,
                 st_ref, qin_ref, kin_ref, kdec_ref, v_ref, g_ref, oa_ref):
    n_chunks = TSH // CHUNK
    hk = HA_HEADS * HA_DK

    @pl.when(pl.program_id(1) == 0)
    def _():
        st_ref[...] = jnp.zeros_like(st_ref)

    hb = _rms(x_ref[0], nw_ref[...]).astype(BF16)

    lbr = lbraw_ref[...]
    rows = [lbr[j:j + 1, :] for j in range(lbr.shape[0])]
    mx = functools.reduce(jnp.maximum, rows)
    es = [jnp.exp(r - mx) for r in rows]
    lb = es[0] / functools.reduce(lambda a, b: a + b, es)

    fl = jnp.dot(hb, wa_ref[:, hk:2 * hk], preferred_element_type=F32)
    f = lb + (1.0 - lb) * jax.nn.sigmoid(fl)
    lf = jnp.log(f)
    k = 1.0 - f
    tri = tri_ref[...]
    bcum = functools.reduce(lambda a, b: a + b,
                            [jnp.dot(tri, p, preferred_element_type=F32) for p in _split3(lf)])
    b_last = [bcum[c * CHUNK + CHUNK - 1:c * CHUNK + CHUNK, :] for c in range(n_chunks)]
    bl_full = jnp.concatenate([jnp.broadcast_to(b, (CHUNK, hk)) for b in b_last], axis=0)

    q = jnp.dot(hb, wa_ref[:, 0:hk], preferred_element_type=F32)
    qin_ref[...] = (jax.nn.silu(q) * jnp.exp(bcum)).astype(BF16)
    kin_ref[...] = (k * jnp.exp(-bcum)).astype(BF16)
    kdec_ref[...] = (k * jnp.exp(bl_full - bcum)).astype(BF16)
    v_ref[...] = jnp.dot(hb, wa_ref[:, 2 * hk:3 * hk], preferred_element_type=F32).astype(BF16)
    g = jnp.dot(hb, wa_ref[:, 3 * hk:4 * hk], preferred_element_type=F32)
    g_ref[...] = jax.nn.silu(g) * gnw_ref[...]

    causal = (lax.broadcasted_iota(jnp.int32, (CHUNK, CHUNK), 0)
              >= lax.broadcasted_iota(jnp.int32, (CHUNK, CHUNK), 1))
    for c in range(n_chunks):
        r = slice(c * CHUNK, (c + 1) * CHUNK)
        edec = jnp.exp(b_last[c])
        for h in range(HA_HEADS):
            l = slice(h * HA_DK, (h + 1) * HA_DK)
            qh, kh, kd, vh = qin_ref[r, l], kin_ref[r, l], kdec_ref[r, l], v_ref[r, l]
            s = lax.dot_general(qh, kh, NT_DIMS, preferred_element_type=F32)
            s = jnp.where(causal, s, 0.0).astype(BF16)
            st = st_ref[h]
            o = (jnp.dot(s, vh, preferred_element_type=F32)
                 + lax.dot_general(qh, st.astype(BF16), NT_DIMS, preferred_element_type=F32))
            st_ref[h] = st * edec[:, l] + lax.dot_general(vh, kd, TN_DIMS, preferred_element_type=F32)
            ms = jnp.mean(o * o, axis=-1, keepdims=True)
            oa_ref[r, l] = (o * lax.rsqrt(ms + EPS) * g_ref[r, l]).astype(BF16)

    ya = jnp.dot(oa_ref[...], wpa_ref[...], preferred_element_type=F32)
    ga = jnp.dot(hb, wga_ref[...], preferred_element_type=F32)
    out_ref[0] = jax.nn.sigmoid(ga) * ya


def _hgrn_call(x, nw, wa, wga, lb_raw, gnw, wpa, tri):
    b, s, d = x.shape
    hk = HA_HEADS * HA_DK
    const = lambda shape: pl.BlockSpec(shape, lambda i, j: (0,) * len(shape))
    return pl.pallas_call(
        _hgrn_kernel,
        grid=(b, s // TSH),
        in_specs=[
            pl.BlockSpec((1, TSH, d), lambda i, j: (i, j, 0)),
            const((1, d)), const((d, 4 * hk)), const((d, d)), const(lb_raw.shape), const((1, hk)),
            const((hk, d)), const((TSH, TSH)),
        ],
        out_specs=pl.BlockSpec((1, TSH, d), lambda i, j: (i, j, 0)),
        out_shape=jax.ShapeDtypeStruct((b, s, d), F32),
        scratch_shapes=[
            pltpu.VMEM((HA_HEADS, HA_DV, HA_DK), F32),
            pltpu.VMEM((TSH, hk), BF16), pltpu.VMEM((TSH, hk), BF16), pltpu.VMEM((TSH, hk), BF16),
            pltpu.VMEM((TSH, hk), BF16), pltpu.VMEM((TSH, hk), F32), pltpu.VMEM((TSH, hk), BF16),
        ],
        compiler_params=pltpu.CompilerParams(
            dimension_semantics=("arbitrary", "arbitrary"), vmem_limit_bytes=VMEM_LIMIT),
        name="hgrn",
    )(x, nw, wa, wga, lb_raw, gnw, wpa, tri)


def _lru_kernel(x_ref, ya_ref, nw_ref, wb_ref, cw_ref, cb_ref, wra_ref, bra_ref, wrx_ref, brx_ref, lam_ref,
                wpb_ref, wout_ref, nffn_ref, wrt_ref, brt_ref, ustr_ref,
                x1_ref, h2_ref, ri_ref, rw_ref, cnt_ref,
                xbuf_ref, a_ref, u_ref, hl_ref, pp_ref, hc_ref, carry_ref):
    seg = TSL // SUBLANES
    pitch = seg + SUBLANES
    w = LRU_WIDTH
    bi, si = pl.program_id(0), pl.program_id(1)

    @pl.when(si == 0)
    def _():
        xbuf_ref[0:SUBLANES, :] = jnp.zeros((SUBLANES, w), F32)
        hc_ref[...] = jnp.zeros_like(hc_ref)

    @pl.when((bi == 0) & (si == 0))
    def _():
        carry_ref[...] = jnp.zeros_like(carry_ref)

    x = x_ref[0]
    hb = _rms(x, nw_ref[...]).astype(BF16)

    xb = jnp.dot(hb, wb_ref[:, 0:w], preferred_element_type=F32)
    xbuf_ref[SUBLANES:SUBLANES + TSL, :] = xb
    cw = cw_ref[...]
    xc = cb_ref[...] + cw[CONV_WIDTH - 1:CONV_WIDTH, :] * xb
    for kk in range(CONV_WIDTH - 1):
        shift = CONV_WIDTH - 1 - kk
        xc = xc + cw[kk:kk + 1, :] * xbuf_ref[SUBLANES - shift:SUBLANES - shift + TSL, :]
    xbuf_ref[0:SUBLANES, :] = xbuf_ref[TSL:TSL + SUBLANES, :]

    xcb = xc.astype(BF16)
    r_parts, i_parts = [], []
    for j in range(LRU_HEADS):
        l = slice(j * LRU_BLOCK, (j + 1) * LRU_BLOCK)
        r_parts.append(jnp.dot(xcb[:, l], wra_ref[j], preferred_element_type=F32))
        i_parts.append(jnp.dot(xcb[:, l], wrx_ref[j], preferred_element_type=F32))
    rg = jax.nn.sigmoid(jnp.concatenate(r_parts, axis=1) + bra_ref[...])
    ig = jax.nn.sigmoid(jnp.concatenate(i_parts, axis=1) + brx_ref[...])
    log_a = RG_C * rg * jax.nn.log_sigmoid(lam_ref[...])
    a = jnp.exp(log_a)
    mult = jnp.sqrt(1.0 - a * a)
    is_first = (lax.broadcasted_iota(jnp.int32, (TSL, 1), 0) == 0) & (si == 0)
    mult = jnp.where(is_first, 1.0, mult)
    u = xc * ig * mult
    n_groups = w // LANES
    group_rows = SUBLANES * pitch
    for gi in range(n_groups):
        for j in range(SUBLANES):
            dst = slice(gi * group_rows + j * pitch, gi * group_rows + j * pitch + seg)
            a_ref[dst, :] = a[j * seg:(j + 1) * seg, gi * LANES:(gi + 1) * LANES]
            u_ref[dst, :] = u[j * seg:(j + 1) * seg, gi * LANES:(gi + 1) * LANES]

    def step(i, hp):
        hs, ps = hp
        new_h, new_p = [], []
        for gi in range(n_groups):
            rows = pl.ds(gi * group_rows + i, SUBLANES, stride=pitch)
            ai = a_ref[rows, :]
            h = ai * hs[gi] + u_ref[rows, :]
            p = ai * ps[gi]
            hl_ref[rows, :] = h
            pp_ref[rows, :] = p
            new_h.append(h)
            new_p.append(p)
        return tuple(new_h), tuple(new_p)

    init = (tuple(jnp.zeros((SUBLANES, LANES), F32) for _ in range(n_groups)),
            tuple(jnp.ones((SUBLANES, LANES), F32) for _ in range(n_groups)))
    h_end, p_end = lax.fori_loop(0, seg, step, init, unroll=4)
    h_end = jnp.concatenate(h_end, axis=1)
    p_end = jnp.concatenate(p_end, axis=1)
    c = hc_ref[...]
    h_parts = []
    for j in range(SUBLANES):
        src = [slice(gi * group_rows + j * pitch, gi * group_rows + j * pitch + seg) for gi in range(n_groups)]
        hl = jnp.concatenate([hl_ref[r, :] for r in src], axis=1)
        pp = jnp.concatenate([pp_ref[r, :] for r in src], axis=1)
        h_parts.append(hl + pp * c)
        c = h_end[j:j + 1, :] + p_end[j:j + 1, :] * c
    hc_ref[...] = c
    h_full = jnp.concatenate(h_parts, axis=0)

    gate_b = jnp.dot(hb, wb_ref[:, w:2 * w], preferred_element_type=F32)
    ob = (h_full * jax.nn.gelu(gate_b)).astype(BF16)
    yb = jnp.dot(ob, wpb_ref[...], preferred_element_type=F32)
    gb = jnp.dot(hb, wb_ref[:, 2 * w:3 * w], preferred_element_type=F32)
    merged = ya_ref[0] + jax.nn.sigmoid(gb) * yb
    x1 = x + jnp.dot(merged.astype(BF16), wout_ref[...], preferred_element_type=F32)
    x1_ref[0] = x1

    h2 = _rms(x1, nffn_ref[...])
    h_hi = h2.astype(BF16)
    h2_ref[0] = h_hi
    h_lo = (h2 - h_hi.astype(F32)).astype(BF16)
    wt = wrt_ref[...]
    w_hi = wt.astype(BF16)
    w_lo = (wt - w_hi.astype(F32)).astype(BF16)
    logits = (lax.dot_general(w_hi, h_hi, NT_DIMS, preferred_element_type=F32)
              + lax.dot_general(w_hi, h_lo, NT_DIMS, preferred_element_type=F32)
              + lax.dot_general(w_lo, h_hi, NT_DIMS, preferred_element_type=F32)) + brt_ref[...]

    eio = lax.broadcasted_iota(jnp.int32, (N_EXPERTS, TSL), 0)
    idxs, vals = [], []
    for _ in range(TOP_K):
        m = jnp.max(logits, axis=0, keepdims=True)
        ik = jnp.min(jnp.where(logits == m, eio, N_EXPERTS), axis=0, keepdims=True)
        idxs.append(ik)
        vals.append(m)
        logits = jnp.where(eio == ik, -jnp.inf, logits)
    exps = [jnp.exp(v - vals[0]) for v in vals]
    den = functools.reduce(lambda p, q: p + q, exps)
    onehots = [(eio == ik) for ik in idxs]
    oh = functools.reduce(lambda p, q: p + q, [o.astype(F32) for o in onehots])
    cum = jnp.dot(oh.astype(BF16), ustr_ref[...], preferred_element_type=F32) + carry_ref[:, 0:1]
    poss = [jnp.sum(jnp.where(o, cum, 0.0), axis=0, keepdims=True).astype(jnp.int32) for o in onehots]
    carry_ref[...] = carry_ref[...] + jnp.sum(oh, axis=1, keepdims=True)
    ri_ref[...] = jnp.concatenate(idxs + poss, axis=0)
    rw_ref[...] = jnp.concatenate([e / den for e in exps] + [jnp.zeros((SUBLANES - TOP_K, TSL), F32)], axis=0)
    cnt_ref[...] = carry_ref[...]


def _lru_call(x, ya, nw, wb, cw, cb, wra, bra, wrx, brx, lam, wpb, wout, nffn, wrt, brt, ustr):
    b, s, d = x.shape
    w = LRU_WIDTH
    scan_shape = ((TSL + SUBLANES * SUBLANES) * (w // LANES), LANES)
    const = lambda shape: pl.BlockSpec(shape, lambda i, j: (0,) * len(shape))
    tok = pl.BlockSpec((1, TSL, d), lambda i, j: (i, j, 0))
    rt = pl.BlockSpec((SUBLANES, TSL), lambda i, j: (0, i * (s // TSL) + j))
    return pl.pallas_call(
        _lru_kernel,
        grid=(b, s // TSL),
        in_specs=[
            tok, tok, const((1, d)), const((d, 3 * w)), const((CONV_WIDTH, w)), const((1, w)),
            const((LRU_HEADS, LRU_BLOCK, LRU_BLOCK)), const((1, w)),
            const((LRU_HEADS, LRU_BLOCK, LRU_BLOCK)), const((1, w)), const((1, w)),
            const((w, d)), const((d, d)), const((1, d)), const((N_EXPERTS, d)), const((N_EXPERTS, 1)),
            const((TSL, TSL)),
        ],
        out_specs=[tok, tok, rt, rt, const((N_EXPERTS, LANES))],
        out_shape=[
            jax.ShapeDtypeStruct((b, s, d), F32),
            jax.ShapeDtypeStruct((b, s, d), BF16),
            jax.ShapeDtypeStruct((SUBLANES, b * s), jnp.int32),
            jax.ShapeDtypeStruct((SUBLANES, b * s), F32),
            jax.ShapeDtypeStruct((N_EXPERTS, LANES), F32),
        ],
        scratch_shapes=[
            pltpu.VMEM((TSL + SUBLANES, w), F32),
            pltpu.VMEM(scan_shape, F32), pltpu.VMEM(scan_shape, F32),
            pltpu.VMEM(scan_shape, F32), pltpu.VMEM(scan_shape, F32),
            pltpu.VMEM((1, w), F32), pltpu.VMEM((N_EXPERTS, LANES), F32),
        ],
        compiler_params=pltpu.CompilerParams(
            dimension_semantics=("arbitrary", "arbitrary"), vmem_limit_bytes=VMEM_LIMIT),
        name="lru_router",
    )(x, ya, nw, wb, cw, cb, wra, bra, wrx, brx, lam, wpb, wout, nffn, wrt, brt, ustr)


def _moe_kernel(te_ref, nu_ref, xs_ref, wgu_ref, bg_ref, bl_ref, wd_ref, bd_ref, perm_ref, y_ref,
                wg_s, wl_s, wd_s):
    i = pl.program_id(0)
    half = MXU_COLS // 2
    new_expert = (i == 0) | (te_ref[i] != te_ref[jnp.maximum(i - 1, 0)])

    @pl.when(new_expert & (i < nu_ref[0]))
    def _():
        for blk in range(2 * D_FF // MXU_COLS):
            wblk = wgu_ref[0, :, blk * MXU_COLS:(blk + 1) * MXU_COLS].astype(BF16)
            pw = jnp.dot(wblk, perm_ref[...], preferred_element_type=F32)
            wg_s[:, blk * half:(blk + 1) * half] = pw[:, :half].astype(BF16)
            wl_s[:, blk * half:(blk + 1) * half] = pw[:, half:].astype(BF16)
        wd_s[...] = wd_ref[0].astype(BF16)

    @pl.when(i < nu_ref[0])
    def _():
        xs = xs_ref[...]
        g = jnp.dot(xs, wg_s[...], preferred_element_type=F32) + bg_ref[0]
        l = jnp.dot(xs, wl_s[...], preferred_element_type=F32) + bl_ref[0]
        g = jnp.minimum(g, SWIGLU_LIMIT)
        l = jnp.clip(l, -SWIGLU_LIMIT, SWIGLU_LIMIT)
        act = g * jax.nn.sigmoid(SWIGLU_ALPHA * g) * (l + 1.0)
        y = jnp.dot(act.astype(BF16), wd_s[...], preferred_element_type=F32) + bd_ref[0]
        y_ref[...] = y.astype(y_ref.dtype)

    @pl.when(i >= nu_ref[0])
    def _():
        y_ref[...] = jnp.zeros_like(y_ref)


def _moe_call(tile_expert, n_used, xs, wgu, bg, bl, wd, bd):
    p, d = xs.shape
    n_tiles = p // TM
    half = MXU_COLS // 2
    r = lax.broadcasted_iota(jnp.int32, (MXU_COLS, MXU_COLS), 0)
    c = lax.broadcasted_iota(jnp.int32, (MXU_COLS, MXU_COLS), 1)
    perm = (r == jnp.where(c < half, 2 * c, 2 * (c - half) + 1)).astype(BF16)
    row = lambda i, te, nu: (jnp.minimum(i, nu[0] - 1), 0)
    ex3 = lambda i, te, nu: (te[i], 0, 0)
    grid_spec = pltpu.PrefetchScalarGridSpec(
        num_scalar_prefetch=2,
        grid=(n_tiles,),
        in_specs=[
            pl.BlockSpec((TM, d), row),
            pl.BlockSpec((1, d, 2 * D_FF), ex3),
            pl.BlockSpec((1, 1, D_FF), ex3), pl.BlockSpec((1, 1, D_FF), ex3),
            pl.BlockSpec((1, D_FF, d), ex3), pl.BlockSpec((1, 1, d), ex3),
            pl.BlockSpec((MXU_COLS, MXU_COLS), lambda i, te, nu: (0, 0)),
        ],
        out_specs=pl.BlockSpec((TM, d), lambda i, te, nu: (i, 0)),
        scratch_shapes=[pltpu.VMEM((d, D_FF), BF16), pltpu.VMEM((d, D_FF), BF16), pltpu.VMEM((D_FF, d), BF16)],
    )
    return pl.pallas_call(
        _moe_kernel,
        grid_spec=grid_spec,
        out_shape=jax.ShapeDtypeStruct((p, d), BF16),
        compiler_params=pltpu.CompilerParams(
            dimension_semantics=("arbitrary",), vmem_limit_bytes=VMEM_LIMIT),
        name="moe_experts",
    )(tile_expert, n_used, xs, wgu, bg, bl, wd, bd, perm)


def _final_kernel(x1_ref, moe_ref, nw_ref, out_ref):
    out_ref[...] = _rms(x1_ref[...] + moe_ref[...], nw_ref[...])


def _final_call(x1, moe, nw):
    t, d = x1.shape
    blk = pl.BlockSpec((TF, d), lambda i: (i, 0))
    return pl.pallas_call(
        _final_kernel,
        grid=(t // TF,),
        in_specs=[blk, blk, pl.BlockSpec((1, d), lambda i: (0, 0))],
        out_specs=blk,
        out_shape=jax.ShapeDtypeStruct((t, d), F32),
        compiler_params=pltpu.CompilerParams(dimension_semantics=("arbitrary",)),
        name="final_norm",
    )(x1, moe, nw)


def _chunk_masks():
    r = lax.broadcasted_iota(jnp.int32, (TSH, TSH), 0)
    c = lax.broadcasted_iota(jnp.int32, (TSH, TSH), 1)
    tri = ((r // CHUNK == c // CHUNK) & (r >= c)).astype(BF16)
    r = lax.broadcasted_iota(jnp.int32, (TSL, TSL), 0)
    c = lax.broadcasted_iota(jnp.int32, (TSL, TSL), 1)
    ustr = (r < c).astype(BF16)
    return tri, ustr


def kernel(x, norm_mix, w_in, lb_raw, gn_w, conv_w, conv_b, w_rg_a, b_rg_a, w_rg_x, b_rg_x, lru_lambda,
           w_proj_a, w_proj_b, w_out, norm_ffn, w_router, b_router, w_gu, b_gu, w_down, b_down, norm_final):
    b, s, d = x.shape
    t = b * s
    assert w_in.shape[0] == 1 and s % TSH == 0 and s % TSL == 0 and t % TF == 0
    hk = HA_HEADS * HA_DK
    w = LRU_WIDTH
    tri, ustr = _chunk_masks()

    wi = w_in[0]
    o_lru = 4 * hk
    wa = wi[:, 0:o_lru].astype(BF16)
    wb = jnp.concatenate([wi[:, o_lru:o_lru + 2 * w], wi[:, o_lru + 2 * w + d:o_lru + 2 * w + 2 * d]],
                         axis=1).astype(BF16)
    wga = wi[:, o_lru + 2 * w:o_lru + 2 * w + d].astype(BF16)
    row = lambda v: v.reshape(1, -1)

    ya = _hgrn_call(x, row(norm_mix[0]), wa, wga, lb_raw, row(gn_w[0]), w_proj_a[0].astype(BF16), tri)
    x1, h2, ri, rw, cnt = _lru_call(
        x, ya, row(norm_mix[0]), wb, conv_w[0], row(conv_b[0]), w_rg_a[0].astype(BF16), row(b_rg_a[0]),
        w_rg_x[0].astype(BF16), row(b_rg_x[0]), row(lru_lambda[0]), w_proj_b[0].astype(BF16),
        w_out[0].astype(BF16), row(norm_ffn[0]), w_router[0].T, b_router[0].reshape(-1, 1), ustr)

    idx, pos, gate_w = ri[0:TOP_K], ri[TOP_K:2 * TOP_K], rw[0:TOP_K]
    counts = cnt[:, 0].astype(jnp.int32)
    padded = ((counts + TM - 1) // TM) * TM
    ends = jnp.cumsum(padded)
    offs = ends - padded
    slot = pos
    for e in range(N_EXPERTS):
        slot = slot + jnp.where(idx == e, offs[e], 0)
    n_rows = TOP_K * t + N_EXPERTS * TM
    n_tiles = n_rows // TM
    tile_expert = jnp.minimum(
        jnp.sum(jnp.arange(n_tiles, dtype=jnp.int32)[:, None] * TM >= ends[None, :], axis=1),
        N_EXPERTS - 1).astype(jnp.int32)
    n_used = (ends[-1:] // TM).astype(jnp.int32)

    tok_of_slot = (jnp.arange(n_rows, dtype=jnp.int32) % t).at[slot.reshape(-1)].set(
        jnp.tile(jnp.arange(t, dtype=jnp.int32), TOP_K), unique_indices=True)
    xs = jnp.take(h2.reshape(t, d), tok_of_slot, axis=0)

    y = _moe_call(
        tile_expert, n_used, xs, w_gu[0],
        b_gu[0][:, 0::2].reshape(N_EXPERTS, 1, D_FF), b_gu[0][:, 1::2].reshape(N_EXPERTS, 1, D_FF),
        w_down[0], b_down[0].reshape(N_EXPERTS, 1, d))

    moe = jnp.sum(gate_w[:, :, None] * jnp.take(y, slot, axis=0).astype(F32), axis=0)
    out = _final_call(x1.reshape(t, d), moe, row(norm_final))
    return out.reshape(b, s, d)
```

```python
import functools

import jax
import jax.numpy as jnp
from jax import lax
from jax.experimental import pallas as pl
from jax.experimental.pallas import tpu as pltpu

F32 = jnp.float32
BF16 = jnp.bfloat16

D_MODEL = 1024
HA_HEADS = 8
HA_DK = 128
HA_DV = 128
CHUNK = 64
LRU_WIDTH = 1024
LRU_HEADS = 4
LRU_BLOCK = LRU_WIDTH // LRU_HEADS
CONV_WIDTH = 4
RG_C = 8.0
N_EXPERTS = 32
TOP_K = 4
D_FF = 1024
SWIGLU_LIMIT = 7.0
SWIGLU_ALPHA = 1.702
EPS = 1e-5

SUBLANES = 8
LANES = 128
MXU_COLS = 256
TSH = 256
TSL = 512
TM = 512
TF = 512
VMEM_LIMIT = 56 * 1024 * 1024

NT_DIMS = (((1,), (1,)), ((), ()))
TN_DIMS = (((0,), (0,)), ((), ()))


def _rms(x, w):
    return x * lax.rsqrt(jnp.mean(x * x, axis=-1, keepdims=True) + EPS) * w


def _split3(v):
    hi = v.astype(BF16)
    r1 = v - hi.astype(F32)
    mid = r1.astype(BF16)
    lo = (r1 - mid.astype(F32)).astype(BF16)
    return hi, mid, lo


def _hgrn_kernel(x_ref, nw_ref, wa_ref, wga_ref, lbraw_ref, gnw_ref, wpa_ref, tri_ref, out_ref,
                 st_ref, qin_ref, kin_ref, kdec_ref, v_ref, g_ref, oa_ref):
    n_chunks = TSH // CHUNK
    hk = HA_HEADS * HA_DK

    @pl.when(pl.program_id(1) == 0)
    def _():
        st_ref[...] = jnp.zeros_like(st_ref)

    hb = _rms(x_ref[0], nw_ref[...]).astype(BF16)

    lbr = lbraw_ref[...]
    rows = [lbr[j:j + 1, :] for j in range(lbr.shape[0])]
    mx = functools.reduce(jnp.maximum, rows)
    es = [jnp.exp(r - mx) for r in rows]
    lb = es[0] / functools.reduce(lambda a, b: a + b, es)

    fl = jnp.dot(hb, wa_ref[:, hk:2 * hk], preferred_element_type=F32)
    f = lb + (1.0 - lb) * jax.nn.sigmoid(fl)
    lf = jnp.log(f)
    k = 1.0 - f
    tri = tri_ref[...]
    bcum = functools.reduce(lambda a, b: a + b,
                            [jnp.dot(tri, p, preferred_element_type=F32) for p in _split3(lf)])
    b_last = [bcum[c * CHUNK + CHUNK - 1:c * CHUNK + CHUNK, :] for c in range(n_chunks)]
    bl_full = jnp.concatenate([jnp.broadcast_to(b, (CHUNK, hk)) for b in b_last], axis=0)

    q = jnp.dot(hb, wa_ref[:, 0:hk], preferred_element_type=F32)
    qin_ref[...] = (jax.nn.silu(q) * jnp.exp(bcum)).astype(BF16)
    kin_ref[...] = (k * jnp.exp(-bcum)).astype(BF16)
    kdec_ref[...] = (k * jnp.exp(bl_full - bcum)).astype(BF16)
    v_ref[...] = jnp.dot(hb, wa_ref[:, 2 * hk:3 * hk], preferred_element_type=F32).astype(BF16)
    g = jnp.dot(hb, wa_ref[:, 3 * hk:4 * hk], preferred_element_type=F32)
    g_ref[...] = jax.nn.silu(g) * gnw_ref[...]

    causal = (lax.broadcasted_iota(jnp.int32, (CHUNK, CHUNK), 0)
              >= lax.broadcasted_iota(jnp.int32, (CHUNK, CHUNK), 1))
    sts = [st_ref[h] for h in range(HA_HEADS)]
    for c in range(n_chunks):
        r = slice(c * CHUNK, (c + 1) * CHUNK)
        edec = jnp.exp(b_last[c])
        for h in range(HA_HEADS):
            l = slice(h * HA_DK, (h + 1) * HA_DK)
            qh, kh, kd, vh = qin_ref[r, l], kin_ref[r, l], kdec_ref[r, l], v_ref[r, l]
            s = lax.dot_general(qh, kh, NT_DIMS, preferred_element_type=F32)
            s = jnp.where(causal, s, 0.0).astype(BF16)
            o = (jnp.dot(s, vh, preferred_element_type=F32)
                 + lax.dot_general(qh, sts[h].astype(BF16), NT_DIMS, preferred_element_type=F32))
            sts[h] = sts[h] * edec[:, l] + lax.dot_general(vh, kd, TN_DIMS, preferred_element_type=F32)
            ms = jnp.mean(o * o, axis=-1, keepdims=True)
            oa_ref[r, l] = (o * lax.rsqrt(ms + EPS) * g_ref[r, l]).astype(BF16)
    for h in range(HA_HEADS):
        st_ref[h] = sts[h]

    ya = jnp.dot(oa_ref[...], wpa_ref[...], preferred_element_type=F32)
    ga = jnp.dot(hb, wga_ref[...], preferred_element_type=F32)
    out_ref[0] = jax.nn.sigmoid(ga) * ya


def _hgrn_call(x, nw, wa, wga, lb_raw, gnw, wpa, tri):
    b, s, d = x.shape
    hk = HA_HEADS * HA_DK
    const = lambda shape: pl.BlockSpec(shape, lambda i, j: (0,) * len(shape))
    return pl.pallas_call(
        _hgrn_kernel,
        grid=(b, s // TSH),
        in_specs=[
            pl.BlockSpec((1, TSH, d), lambda i, j: (i, j, 0)),
            const((1, d)), const((d, 4 * hk)), const((d, d)), const(lb_raw.shape), const((1, hk)),
            const((hk, d)), const((TSH, TSH)),
        ],
        out_specs=pl.BlockSpec((1, TSH, d), lambda i, j: (i, j, 0)),
        out_shape=jax.ShapeDtypeStruct((b, s, d), F32),
        scratch_shapes=[
            pltpu.VMEM((HA_HEADS, HA_DV, HA_DK), F32),
            pltpu.VMEM((TSH, hk), BF16), pltpu.VMEM((TSH, hk), BF16), pltpu.VMEM((TSH, hk), BF16),
            pltpu.VMEM((TSH, hk), BF16), pltpu.VMEM((TSH, hk), F32), pltpu.VMEM((TSH, hk), BF16),
        ],
        compiler_params=pltpu.CompilerParams(
            dimension_semantics=("arbitrary", "arbitrary"), vmem_limit_bytes=VMEM_LIMIT),
        name="hgrn",
    )(x, nw, wa, wga, lb_raw, gnw, wpa, tri)


def _lru_kernel(x_ref, ya_ref, nw_ref, wb_ref, cw_ref, cb_ref, wra_ref, bra_ref, wrx_ref, brx_ref, lam_ref,
                wpb_ref, wout_ref, nffn_ref, wrt_ref, brt_ref, ustr_ref,
                x1_ref, h2_ref, ri_ref, rw_ref, cnt_ref,
                xbuf_ref, a_ref, u_ref, hl_ref, pp_ref, hc_ref, carry_ref):
    seg = TSL // SUBLANES
    pitch = seg + SUBLANES
    w = LRU_WIDTH
    bi, si = pl.program_id(0), pl.program_id(1)

    @pl.when(si == 0)
    def _():
        xbuf_ref[0:SUBLANES, :] = jnp.zeros((SUBLANES, w), F32)
        hc_ref[...] = jnp.zeros_like(hc_ref)

    @pl.when((bi == 0) & (si == 0))
    def _():
        carry_ref[...] = jnp.zeros_like(carry_ref)

    x = x_ref[0]
    hb = _rms(x, nw_ref[...]).astype(BF16)

    xb = jnp.dot(hb, wb_ref[:, 0:w], preferred_element_type=F32)
    xbuf_ref[SUBLANES:SUBLANES + TSL, :] = xb
    cw = cw_ref[...]
    xc = cb_ref[...] + cw[CONV_WIDTH - 1:CONV_WIDTH, :] * xb
    for kk in range(CONV_WIDTH - 1):
        shift = CONV_WIDTH - 1 - kk
        xc = xc + cw[kk:kk + 1, :] * xbuf_ref[SUBLANES - shift:SUBLANES - shift + TSL, :]
    xbuf_ref[0:SUBLANES, :] = xbuf_ref[TSL:TSL + SUBLANES, :]

    xcb = xc.astype(BF16)
    r_parts, i_parts = [], []
    for j in range(LRU_HEADS):
        l = slice(j * LRU_BLOCK, (j + 1) * LRU_BLOCK)
        r_parts.append(jnp.dot(xcb[:, l], wra_ref[j], preferred_element_type=F32))
        i_parts.append(jnp.dot(xcb[:, l], wrx_ref[j], preferred_element_type=F32))
    rg = jax.nn.sigmoid(jnp.concatenate(r_parts, axis=1) + bra_ref[...])
    ig = jax.nn.sigmoid(jnp.concatenate(i_parts, axis=1) + brx_ref[...])
    log_a = RG_C * rg * jax.nn.log_sigmoid(lam_ref[...])
    a = jnp.exp(log_a)
    mult = jnp.sqrt(1.0 - a * a)
    is_first = (lax.broadcasted_iota(jnp.int32, (TSL, 1), 0) == 0) & (si == 0)
    mult = jnp.where(is_first, 1.0, mult)
    u = xc * ig * mult
    n_groups = w // LANES
    group_rows = SUBLANES * pitch
    for gi in range(n_groups):
        for j in range(SUBLANES):
            dst = slice(gi * group_rows + j * pitch, gi * group_rows + j * pitch + seg)
            a_ref[dst, :] = a[j * seg:(j + 1) * seg, gi * LANES:(gi + 1) * LANES]
            u_ref[dst, :] = u[j * seg:(j + 1) * seg, gi * LANES:(gi + 1) * LANES]

    def step(i, hp):
        hs, ps = hp
        new_h, new_p = [], []
        for gi in range(n_groups):
            rows = pl.ds(gi * group_rows + i, SUBLANES, stride=pitch)
            ai = a_ref[rows, :]
            h = ai * hs[gi] + u_ref[rows, :]
            p = ai * ps[gi]
            hl_ref[rows, :] = h
            pp_ref[rows, :] = p
            new_h.append(h)
            new_p.append(p)
        return tuple(new_h), tuple(new_p)

    init = (tuple(jnp.zeros((SUBLANES, LANES), F32) for _ in range(n_groups)),
            tuple(jnp.ones((SUBLANES, LANES), F32) for _ in range(n_groups)))
    h_end, p_end = lax.fori_loop(0, seg, step, init, unroll=4)
    h_end = jnp.concatenate(h_end, axis=1)
    p_end = jnp.concatenate(p_end, axis=1)
    c = hc_ref[...]
    h_parts = []
    for j in range(SUBLANES):
        src = [slice(gi * group_rows + j * pitch, gi * group_rows + j * pitch + seg) for gi in range(n_groups)]
        hl = jnp.concatenate([hl_ref[r, :] for r in src], axis=1)
        pp = jnp.concatenate([pp_ref[r, :] for r in src], axis=1)
        h_parts.append(hl + pp * c)
        c = h_end[j:j + 1, :] + p_end[j:j + 1, :] * c
    hc_ref[...] = c
    h_full = jnp.concatenate(h_parts, axis=0)

    gate_b = jnp.dot(hb, wb_ref[:, w:2 * w], preferred_element_type=F32)
    ob = (h_full * jax.nn.gelu(gate_b)).astype(BF16)
    yb = jnp.dot(ob, wpb_ref[...], preferred_element_type=F32)
    gb = jnp.dot(hb, wb_ref[:, 2 * w:3 * w], preferred_element_type=F32)
    merged = ya_ref[0] + jax.nn.sigmoid(gb) * yb
    x1 = x + jnp.dot(merged.astype(BF16), wout_ref[...], preferred_element_type=F32)
    x1_ref[0] = x1

    h2 = _rms(x1, nffn_ref[...])
    h_hi = h2.astype(BF16)
    h2_ref[0] = h_hi
    h_lo = (h2 - h_hi.astype(F32)).astype(BF16)
    wt = wrt_ref[...]
    w_hi = wt.astype(BF16)
    w_lo = (wt - w_hi.astype(F32)).astype(BF16)
    logits = (lax.dot_general(w_hi, h_hi, NT_DIMS, preferred_element_type=F32)
              + lax.dot_general(w_hi, h_lo, NT_DIMS, preferred_element_type=F32)
              + lax.dot_general(w_lo, h_hi, NT_DIMS, preferred_element_type=F32)) + brt_ref[...]

    eio = lax.broadcasted_iota(jnp.int32, (N_EXPERTS, TSL), 0)
    idxs, vals = [], []
    for _ in range(TOP_K):
        m = jnp.max(logits, axis=0, keepdims=True)
        ik = jnp.min(jnp.where(logits == m, eio, N_EXPERTS), axis=0, keepdims=True)
        idxs.append(ik)
        vals.append(m)
        logits = jnp.where(eio == ik, -jnp.inf, logits)
    exps = [jnp.exp(v - vals[0]) for v in vals]
    den = functools.reduce(lambda p, q: p + q, exps)
    onehots = [(eio == ik) for ik in idxs]
    oh = functools.reduce(lambda p, q: p + q, [o.astype(F32) for o in onehots])
    cum = jnp.dot(oh.astype(BF16), ustr_ref[...], preferred_element_type=F32) + carry_ref[:, 0:1]
    poss = [jnp.sum(jnp.where(o, cum, 0.0), axis=0, keepdims=True).astype(jnp.int32) for o in onehots]
    carry_ref[...] = carry_ref[...] + jnp.sum(oh, axis=1, keepdims=True)
    ri_ref[...] = jnp.concatenate(idxs + poss, axis=0)
    rw_ref[...] = jnp.concatenate([e / den for e in exps] + [jnp.zeros((SUBLANES - TOP_K, TSL), F32)], axis=0)
    cnt_ref[...] = carry_ref[...]


def _lru_call(x, ya, nw, wb, cw, cb, wra, bra, wrx, brx, lam, wpb, wout, nffn, wrt, brt, ustr):
    b, s, d = x.shape
    w = LRU_WIDTH
    scan_shape = ((TSL + SUBLANES * SUBLANES) * (w // LANES), LANES)
    const = lambda shape: pl.BlockSpec(shape, lambda i, j: (0,) * len(shape))
    tok = pl.BlockSpec((1, TSL, d), lambda i, j: (i, j, 0))
    rt = pl.BlockSpec((SUBLANES, TSL), lambda i, j: (0, i * (s // TSL) + j))
    return pl.pallas_call(
        _lru_kernel,
        grid=(b, s // TSL),
        in_specs=[
            tok, tok, const((1, d)), const((d, 3 * w)), const((CONV_WIDTH, w)), const((1, w)),
            const((LRU_HEADS, LRU_BLOCK, LRU_BLOCK)), const((1, w)),
            const((LRU_HEADS, LRU_BLOCK, LRU_BLOCK)), const((1, w)), const((1, w)),
            const((w, d)), const((d, d)), const((1, d)), const((N_EXPERTS, d)), const((N_EXPERTS, 1)),
            const((TSL, TSL)),
        ],
        out_specs=[tok, tok, rt, rt, const((N_EXPERTS, LANES))],
        out_shape=[
            jax.ShapeDtypeStruct((b, s, d), F32),
            jax.ShapeDtypeStruct((b, s, d), BF16),
            jax.ShapeDtypeStruct((SUBLANES, b * s), jnp.int32),
            jax.ShapeDtypeStruct((SUBLANES, b * s), F32),
            jax.ShapeDtypeStruct((N_EXPERTS, LANES), F32),
        ],
        scratch_shapes=[
            pltpu.VMEM((TSL + SUBLANES, w), F32),
            pltpu.VMEM(scan_shape, F32), pltpu.VMEM(scan_shape, F32),
            pltpu.VMEM(scan_shape, F32), pltpu.VMEM(scan_shape, F32),
            pltpu.VMEM((1, w), F32), pltpu.VMEM((N_EXPERTS, LANES), F32),
        ],
        compiler_params=pltpu.CompilerParams(
            dimension_semantics=("arbitrary", "arbitrary"), vmem_limit_bytes=VMEM_LIMIT),
        name="lru_router",
    )(x, ya, nw, wb, cw, cb, wra, bra, wrx, brx, lam, wpb, wout, nffn, wrt, brt, ustr)


def _moe_kernel(te_ref, nu_ref, xs_ref, wgu_ref, bg_ref, bl_ref, wd_ref, bd_ref, perm_ref, y_ref,
                wg_s, wl_s, wd_s):
    i = pl.program_id(0)
    half = MXU_COLS // 2
    new_expert = (i == 0) | (te_ref[i] != te_ref[jnp.maximum(i - 1, 0)])

    @pl.when(new_expert & (i < nu_ref[0]))
    def _():
        for blk in range(2 * D_FF // MXU_COLS):
            wblk = wgu_ref[0, :, blk * MXU_COLS:(blk + 1) * MXU_COLS].astype(BF16)
            pw = jnp.dot(wblk, perm_ref[...], preferred_element_type=F32)
            wg_s[:, blk * half:(blk + 1) * half] = pw[:, :half].astype(BF16)
            wl_s[:, blk * half:(blk + 1) * half] = pw[:, half:].astype(BF16)
        wd_s[...] = wd_ref[0].astype(BF16)

    @pl.when(i < nu_ref[0])
    def _():
        xs = xs_ref[...]
        g = jnp.dot(xs, wg_s[...], preferred_element_type=F32) + bg_ref[0]
        l = jnp.dot(xs, wl_s[...], preferred_element_type=F32) + bl_ref[0]
        g = jnp.minimum(g, SWIGLU_LIMIT)
        l = jnp.clip(l, -SWIGLU_LIMIT, SWIGLU_LIMIT)
        act = g * jax.nn.sigmoid(SWIGLU_ALPHA * g) * (l + 1.0)
        y = jnp.dot(act.astype(BF16), wd_s[...], preferred_element_type=F32) + bd_ref[0]
        y_ref[...] = y.astype(y_ref.dtype)

    @pl.when(i >= nu_ref[0])
    def _():
        y_ref[...] = jnp.zeros_like(y_ref)


def _moe_call(tile_expert, n_used, xs, wgu, bg, bl, wd, bd):
    p, d = xs.shape
    n_tiles = p // TM
    half = MXU_COLS // 2
    r = lax.broadcasted_iota(jnp.int32, (MXU_COLS, MXU_COLS), 0)
    c = lax.broadcasted_iota(jnp.int32, (MXU_COLS, MXU_COLS), 1)
    perm = (r == jnp.where(c < half, 2 * c, 2 * (c - half) + 1)).astype(BF16)
    row = lambda i, te, nu: (jnp.minimum(i, nu[0] - 1), 0)
    ex3 = lambda i, te, nu: (te[i], 0, 0)
    grid_spec = pltpu.PrefetchScalarGridSpec(
        num_scalar_prefetch=2,
        grid=(n_tiles,),
        in_specs=[
            pl.BlockSpec((TM, d), row),
            pl.BlockSpec((1, d, 2 * D_FF), ex3),
            pl.BlockSpec((1, 1, D_FF), ex3), pl.BlockSpec((1, 1, D_FF), ex3),
            pl.BlockSpec((1, D_FF, d), ex3), pl.BlockSpec((1, 1, d), ex3),
            pl.BlockSpec((MXU_COLS, MXU_COLS), lambda i, te, nu: (0, 0)),
        ],
        out_specs=pl.BlockSpec((TM, d), lambda i, te, nu: (i, 0)),
        scratch_shapes=[pltpu.VMEM((d, D_FF), BF16), pltpu.VMEM((d, D_FF), BF16), pltpu.VMEM((D_FF, d), BF16)],
    )
    return pl.pallas_call(
        _moe_kernel,
        grid_spec=grid_spec,
        out_shape=jax.ShapeDtypeStruct((p, d), BF16),
        compiler_params=pltpu.CompilerParams(
            dimension_semantics=("arbitrary",), vmem_limit_bytes=VMEM_LIMIT),
        name="moe_experts",
    )(tile_expert, n_used, xs, wgu, bg, bl, wd, bd, perm)


def _final_kernel(x1_ref, yg_ref, gw_ref, nw_ref, out_ref):
    gw = gw_ref[...]
    x2 = x1_ref[...]
    for k in range(TOP_K):
        x2 = x2 + gw[:, k:k + 1] * yg_ref[k].astype(F32)
    out_ref[...] = _rms(x2, nw_ref[...])


def _final_call(x1, yg, gw, nw):
    t, d = x1.shape
    blk = pl.BlockSpec((TF, d), lambda i: (i, 0))
    return pl.pallas_call(
        _final_kernel,
        grid=(t // TF,),
        in_specs=[blk, pl.BlockSpec((TOP_K, TF, d), lambda i: (0, i, 0)),
                  pl.BlockSpec((TF, TOP_K), lambda i: (i, 0)), pl.BlockSpec((1, d), lambda i: (0, 0))],
        out_specs=blk,
        out_shape=jax.ShapeDtypeStruct((t, d), F32),
        compiler_params=pltpu.CompilerParams(dimension_semantics=("arbitrary",)),
        name="final_norm",
    )(x1, yg, gw, nw)


def _chunk_masks():
    r = lax.broadcasted_iota(jnp.int32, (TSH, TSH), 0)
    c = lax.broadcasted_iota(jnp.int32, (TSH, TSH), 1)
    tri = ((r // CHUNK == c // CHUNK) & (r >= c)).astype(BF16)
    r = lax.broadcasted_iota(jnp.int32, (TSL, TSL), 0)
    c = lax.broadcasted_iota(jnp.int32, (TSL, TSL), 1)
    ustr = (r < c).astype(BF16)
    return tri, ustr


def kernel(x, norm_mix, w_in, lb_raw, gn_w, conv_w, conv_b, w_rg_a, b_rg_a, w_rg_x, b_rg_x, lru_lambda,
           w_proj_a, w_proj_b, w_out, norm_ffn, w_router, b_router, w_gu, b_gu, w_down, b_down, norm_final):
    b, s, d = x.shape
    t = b * s
    assert w_in.shape[0] == 1 and s % TSH == 0 and s % TSL == 0 and t % TF == 0
    hk = HA_HEADS * HA_DK
    w = LRU_WIDTH
    tri, ustr = _chunk_masks()

    wi = w_in[0]
    o_lru = 4 * hk
    wa = wi[:, 0:o_lru].astype(BF16)
    wb = jnp.concatenate([wi[:, o_lru:o_lru + 2 * w], wi[:, o_lru + 2 * w + d:o_lru + 2 * w + 2 * d]],
                         axis=1).astype(BF16)
    wga = wi[:, o_lru + 2 * w:o_lru + 2 * w + d].astype(BF16)
    row = lambda v: v.reshape(1, -1)

    ya = _hgrn_call(x, row(norm_mix[0]), wa, wga, lb_raw, row(gn_w[0]), w_proj_a[0].astype(BF16), tri)
    x1, h2, ri, rw, cnt = _lru_call(
        x, ya, row(norm_mix[0]), wb, conv_w[0], row(conv_b[0]), w_rg_a[0].astype(BF16), row(b_rg_a[0]),
        w_rg_x[0].astype(BF16), row(b_rg_x[0]), row(lru_lambda[0]), w_proj_b[0].astype(BF16),
        w_out[0].astype(BF16), row(norm_ffn[0]), w_router[0].T, b_router[0].reshape(-1, 1), ustr)

    idx, pos, gate_w = ri[0:TOP_K], ri[TOP_K:2 * TOP_K], rw[0:TOP_K]
    counts = cnt[:, 0].astype(jnp.int32)
    padded = ((counts + TM - 1) // TM) * TM
    ends = jnp.cumsum(padded)
    offs = ends - padded
    slot = pos
    for e in range(N_EXPERTS):
        slot = slot + jnp.where(idx == e, offs[e], 0)
    n_rows = TOP_K * t + N_EXPERTS * TM
    n_tiles = n_rows // TM
    tile_expert = jnp.minimum(
        jnp.sum(jnp.arange(n_tiles, dtype=jnp.int32)[:, None] * TM >= ends[None, :], axis=1),
        N_EXPERTS - 1).astype(jnp.int32)
    n_used = (ends[-1:] // TM).astype(jnp.int32)

    tok_of_slot = (jnp.arange(n_rows, dtype=jnp.int32) % t).at[slot.reshape(-1)].set(
        jnp.tile(jnp.arange(t, dtype=jnp.int32), TOP_K), unique_indices=True)
    xs = h2.reshape(t, d).at[tok_of_slot].get(mode="promise_in_bounds")

    y = _moe_call(
        tile_expert, n_used, xs, w_gu[0],
        b_gu[0][:, 0::2].reshape(N_EXPERTS, 1, D_FF), b_gu[0][:, 1::2].reshape(N_EXPERTS, 1, D_FF),
        w_down[0], b_down[0].reshape(N_EXPERTS, 1, d))

    yg = y.at[slot].get(mode="promise_in_bounds")
    out = _final_call(x1.reshape(t, d), yg, gate_w.T, row(norm_final))
    return out.reshape(b, s, d)
```

```python
import functools

import jax
import jax.numpy as jnp
from jax import lax
from jax.experimental import pallas as pl
from jax.experimental.pallas import tpu as pltpu
from jax.experimental.pallas import tpu_sc as plsc

F32 = jnp.float32
BF16 = jnp.bfloat16

D_MODEL = 1024
HA_HEADS = 8
HA_DK = 128
HA_DV = 128
CHUNK = 64
LRU_WIDTH = 1024
LRU_HEADS = 4
LRU_BLOCK = LRU_WIDTH // LRU_HEADS
CONV_WIDTH = 4
RG_C = 8.0
N_EXPERTS = 32
TOP_K = 4
D_FF = 1024
SWIGLU_LIMIT = 7.0
SWIGLU_ALPHA = 1.702
EPS = 1e-5

SUBLANES = 8
LANES = 128
MXU_COLS = 256
TSH = 256
TSL = 512
TM = 512
TF = 512
VMEM_LIMIT = 56 * 1024 * 1024
SC_WINDOW = 128
SC_ROW = 128

NT_DIMS = (((1,), (1,)), ((), ()))
TN_DIMS = (((0,), (0,)), ((), ()))


def _rms(x, w):
    return x * lax.rsqrt(jnp.mean(x * x, axis=-1, keepdims=True) + EPS) * w


def _split3(v):
    hi = v.astype(BF16)
    r1 = v - hi.astype(F32)
    mid = r1.astype(BF16)
    lo = (r1 - mid.astype(F32)).astype(BF16)
    return hi, mid, lo


def _hgrn_kernel(x_ref, nw_ref, wa_ref, wga_ref, lbraw_ref, gnw_ref, wpa_ref, tri_ref, out_ref,
                 st_ref, qin_ref, kin_ref, kdec_ref, v_ref, g_ref, oa_ref):
    n_chunks = TSH // CHUNK
    hk = HA_HEADS * HA_DK

    @pl.when(pl.program_id(1) == 0)
    def _():
        st_ref[...] = jnp.zeros_like(st_ref)

    hb = _rms(x_ref[0], nw_ref[...]).astype(BF16)

    lbr = lbraw_ref[...]
    rows = [lbr[j:j + 1, :] for j in range(lbr.shape[0])]
    mx = functools.reduce(jnp.maximum, rows)
    es = [jnp.exp(r - mx) for r in rows]
    lb = es[0] / functools.reduce(lambda a, b: a + b, es)

    fl = jnp.dot(hb, wa_ref[:, hk:2 * hk], preferred_element_type=F32)
    f = lb + (1.0 - lb) * jax.nn.sigmoid(fl)
    lf = jnp.log(f)
    k = 1.0 - f
    tri = tri_ref[...]
    bcum = functools.reduce(lambda a, b: a + b,
                            [jnp.dot(tri, p, preferred_element_type=F32) for p in _split3(lf)])
    b_last = [bcum[c * CHUNK + CHUNK - 1:c * CHUNK + CHUNK, :] for c in range(n_chunks)]
    bl_full = jnp.concatenate([jnp.broadcast_to(b, (CHUNK, hk)) for b in b_last], axis=0)

    q = jnp.dot(hb, wa_ref[:, 0:hk], preferred_element_type=F32)
    qin_ref[...] = (jax.nn.silu(q) * jnp.exp(bcum)).astype(BF16)
    kin_ref[...] = (k * jnp.exp(-bcum)).astype(BF16)
    kdec_ref[...] = (k * jnp.exp(bl_full - bcum)).astype(BF16)
    v_ref[...] = jnp.dot(hb, wa_ref[:, 2 * hk:3 * hk], preferred_element_type=F32).astype(BF16)
    g = jnp.dot(hb, wa_ref[:, 3 * hk:4 * hk], preferred_element_type=F32)
    g_ref[...] = jax.nn.silu(g) * gnw_ref[...]

    causal = (lax.broadcasted_iota(jnp.int32, (CHUNK, CHUNK), 0)
              >= lax.broadcasted_iota(jnp.int32, (CHUNK, CHUNK), 1))
    sts = [st_ref[h] for h in range(HA_HEADS)]
    for c in range(n_chunks):
        r = slice(c * CHUNK, (c + 1) * CHUNK)
        edec = jnp.exp(b_last[c])
        for h in range(HA_HEADS):
            l = slice(h * HA_DK, (h + 1) * HA_DK)
            qh, kh, kd, vh = qin_ref[r, l], kin_ref[r, l], kdec_ref[r, l], v_ref[r, l]
            s = lax.dot_general(qh, kh, NT_DIMS, preferred_element_type=F32)
            s = jnp.where(causal, s, 0.0).astype(BF16)
            o = (jnp.dot(s, vh, preferred_element_type=F32)
                 + lax.dot_general(qh, sts[h].astype(BF16), NT_DIMS, preferred_element_type=F32))
            sts[h] = sts[h] * edec[:, l] + lax.dot_general(vh, kd, TN_DIMS, preferred_element_type=F32)
            ms = jnp.mean(o * o, axis=-1, keepdims=True)
            oa_ref[r, l] = (o * lax.rsqrt(ms + EPS) * g_ref[r, l]).astype(BF16)
    for h in range(HA_HEADS):
        st_ref[h] = sts[h]

    ya = jnp.dot(oa_ref[...], wpa_ref[...], preferred_element_type=F32)
    ga = jnp.dot(hb, wga_ref[...], preferred_element_type=F32)
    out_ref[0] = jax.nn.sigmoid(ga) * ya


def _hgrn_call(x, nw, wa, wga, lb_raw, gnw, wpa, tri):
    b, s, d = x.shape
    hk = HA_HEADS * HA_DK
    const = lambda shape: pl.BlockSpec(shape, lambda i, j: (0,) * len(shape))
    return pl.pallas_call(
        _hgrn_kernel,
        grid=(b, s // TSH),
        in_specs=[
            pl.BlockSpec((1, TSH, d), lambda i, j: (i, j, 0)),
            const((1, d)), const((d, 4 * hk)), const((d, d)), const(lb_raw.shape), const((1, hk)),
            const((hk, d)), const((TSH, TSH)),
        ],
        out_specs=pl.BlockSpec((1, TSH, d), lambda i, j: (i, j, 0)),
        out_shape=jax.ShapeDtypeStruct((b, s, d), F32),
        scratch_shapes=[
            pltpu.VMEM((HA_HEADS, HA_DV, HA_DK), F32),
            pltpu.VMEM((TSH, hk), BF16), pltpu.VMEM((TSH, hk), BF16), pltpu.VMEM((TSH, hk), BF16),
            pltpu.VMEM((TSH, hk), BF16), pltpu.VMEM((TSH, hk), F32), pltpu.VMEM((TSH, hk), BF16),
        ],
        compiler_params=pltpu.CompilerParams(
            dimension_semantics=("arbitrary", "arbitrary"), vmem_limit_bytes=VMEM_LIMIT),
        name="hgrn",
    )(x, nw, wa, wga, lb_raw, gnw, wpa, tri)


def _lru_kernel(x_ref, ya_ref, nw_ref, wb_ref, cw_ref, cb_ref, wra_ref, bra_ref, wrx_ref, brx_ref, lam_ref,
                wpb_ref, wout_ref, nffn_ref, wrt_ref, brt_ref, ustr_ref,
                x1_ref, h2_ref, ri_ref, rw_ref, cnt_ref,
                xbuf_ref, a_ref, u_ref, hl_ref, pp_ref, hc_ref, carry_ref):
    seg = TSL // SUBLANES
    pitch = seg + SUBLANES
    w = LRU_WIDTH
    bi, si = pl.program_id(0), pl.program_id(1)

    @pl.when(si == 0)
    def _():
        xbuf_ref[0:SUBLANES, :] = jnp.zeros((SUBLANES, w), F32)
        hc_ref[...] = jnp.zeros_like(hc_ref)

    @pl.when((bi == 0) & (si == 0))
    def _():
        carry_ref[...] = jnp.zeros_like(carry_ref)

    x = x_ref[0]
    hb = _rms(x, nw_ref[...]).astype(BF16)

    xb = jnp.dot(hb, wb_ref[:, 0:w], preferred_element_type=F32)
    xbuf_ref[SUBLANES:SUBLANES + TSL, :] = xb
    cw = cw_ref[...]
    xc = cb_ref[...] + cw[CONV_WIDTH - 1:CONV_WIDTH, :] * xb
    for kk in range(CONV_WIDTH - 1):
        shift = CONV_WIDTH - 1 - kk
        xc = xc + cw[kk:kk + 1, :] * xbuf_ref[SUBLANES - shift:SUBLANES - shift + TSL, :]
    xbuf_ref[0:SUBLANES, :] = xbuf_ref[TSL:TSL + SUBLANES, :]

    xcb = xc.astype(BF16)
    r_parts, i_parts = [], []
    for j in range(LRU_HEADS):
        l = slice(j * LRU_BLOCK, (j + 1) * LRU_BLOCK)
        r_parts.append(jnp.dot(xcb[:, l], wra_ref[j], preferred_element_type=F32))
        i_parts.append(jnp.dot(xcb[:, l], wrx_ref[j], preferred_element_type=F32))
    rg = jax.nn.sigmoid(jnp.concatenate(r_parts, axis=1) + bra_ref[...])
    ig = jax.nn.sigmoid(jnp.concatenate(i_parts, axis=1) + brx_ref[...])
    log_a = RG_C * rg * jax.nn.log_sigmoid(lam_ref[...])
    a = jnp.exp(log_a)
    mult = jnp.sqrt(1.0 - a * a)
    is_first = (lax.broadcasted_iota(jnp.int32, (TSL, 1), 0) == 0) & (si == 0)
    mult = jnp.where(is_first, 1.0, mult)
    u = xc * ig * mult
    n_groups = w // LANES
    group_rows = SUBLANES * pitch
    for gi in range(n_groups):
        for j in range(SUBLANES):
            dst = slice(gi * group_rows + j * pitch, gi * group_rows + j * pitch + seg)
            a_ref[dst, :] = a[j * seg:(j + 1) * seg, gi * LANES:(gi + 1) * LANES]
            u_ref[dst, :] = u[j * seg:(j + 1) * seg, gi * LANES:(gi + 1) * LANES]

    def step(i, hp):
        hs, ps = hp
        new_h, new_p = [], []
        for gi in range(n_groups):
            rows = pl.ds(gi * group_rows + i, SUBLANES, stride=pitch)
            ai = a_ref[rows, :]
            h = ai * hs[gi] + u_ref[rows, :]
            p = ai * ps[gi]
            hl_ref[rows, :] = h
            pp_ref[rows, :] = p
            new_h.append(h)
            new_p.append(p)
        return tuple(new_h), tuple(new_p)

    init = (tuple(jnp.zeros((SUBLANES, LANES), F32) for _ in range(n_groups)),
            tuple(jnp.ones((SUBLANES, LANES), F32) for _ in range(n_groups)))
    h_end, p_end = lax.fori_loop(0, seg, step, init, unroll=4)
    h_end = jnp.concatenate(h_end, axis=1)
    p_end = jnp.concatenate(p_end, axis=1)
    c = hc_ref[...]
    h_parts = []
    for j in range(SUBLANES):
        src = [slice(gi * group_rows + j * pitch, gi * group_rows + j * pitch + seg) for gi in range(n_groups)]
        hl = jnp.concatenate([hl_ref[r, :] for r in src], axis=1)
        pp = jnp.concatenate([pp_ref[r, :] for r in src], axis=1)
        h_parts.append(hl + pp * c)
        c = h_end[j:j + 1, :] + p_end[j:j + 1, :] * c
    hc_ref[...] = c
    h_full = jnp.concatenate(h_parts, axis=0)

    gate_b = jnp.dot(hb, wb_ref[:, w:2 * w], preferred_element_type=F32)
    ob = (h_full * jax.nn.gelu(gate_b)).astype(BF16)
    yb = jnp.dot(ob, wpb_ref[...], preferred_element_type=F32)
    gb = jnp.dot(hb, wb_ref[:, 2 * w:3 * w], preferred_element_type=F32)
    merged = ya_ref[0] + jax.nn.sigmoid(gb) * yb
    x1 = x + jnp.dot(merged.astype(BF16), wout_ref[...], preferred_element_type=F32)
    x1_ref[0] = x1

    h2 = _rms(x1, nffn_ref[...])
    h_hi = h2.astype(BF16)
    h2_ref[0] = h_hi
    h_lo = (h2 - h_hi.astype(F32)).astype(BF16)
    wt = wrt_ref[...]
    w_hi = wt.astype(BF16)
    w_lo = (wt - w_hi.astype(F32)).astype(BF16)
    logits = (lax.dot_general(w_hi, h_hi, NT_DIMS, preferred_element_type=F32)
              + lax.dot_general(w_hi, h_lo, NT_DIMS, preferred_element_type=F32)
              + lax.dot_general(w_lo, h_hi, NT_DIMS, preferred_element_type=F32)) + brt_ref[...]

    eio = lax.broadcasted_iota(jnp.int32, (N_EXPERTS, TSL), 0)
    idxs, vals = [], []
    for _ in range(TOP_K):
        m = jnp.max(logits, axis=0, keepdims=True)
        ik = jnp.min(jnp.where(logits == m, eio, N_EXPERTS), axis=0, keepdims=True)
        idxs.append(ik)
        vals.append(m)
        logits = jnp.where(eio == ik, -jnp.inf, logits)
    exps = [jnp.exp(v - vals[0]) for v in vals]
    den = functools.reduce(lambda p, q: p + q, exps)
    onehots = [(eio == ik) for ik in idxs]
    oh = functools.reduce(lambda p, q: p + q, [o.astype(F32) for o in onehots])
    cum = jnp.dot(oh.astype(BF16), ustr_ref[...], preferred_element_type=F32) + carry_ref[:, 0:1]
    poss = [jnp.sum(jnp.where(o, cum, 0.0), axis=0, keepdims=True).astype(jnp.int32) for o in onehots]
    carry_ref[...] = carry_ref[...] + jnp.sum(oh, axis=1, keepdims=True)
    ri_ref[...] = jnp.concatenate(idxs + poss, axis=0)
    rw_ref[...] = jnp.concatenate([e / den for e in exps] + [jnp.zeros((SUBLANES - TOP_K, TSL), F32)], axis=0)
    cnt_ref[...] = carry_ref[...]


def _lru_call(x, ya, nw, wb, cw, cb, wra, bra, wrx, brx, lam, wpb, wout, nffn, wrt, brt, ustr):
    b, s, d = x.shape
    w = LRU_WIDTH
    scan_shape = ((TSL + SUBLANES * SUBLANES) * (w // LANES), LANES)
    const = lambda shape: pl.BlockSpec(shape, lambda i, j: (0,) * len(shape))
    tok = pl.BlockSpec((1, TSL, d), lambda i, j: (i, j, 0))
    rt = pl.BlockSpec((SUBLANES, TSL), lambda i, j: (0, i * (s // TSL) + j))
    return pl.pallas_call(
        _lru_kernel,
        grid=(b, s // TSL),
        in_specs=[
            tok, tok, const((1, d)), const((d, 3 * w)), const((CONV_WIDTH, w)), const((1, w)),
            const((LRU_HEADS, LRU_BLOCK, LRU_BLOCK)), const((1, w)),
            const((LRU_HEADS, LRU_BLOCK, LRU_BLOCK)), const((1, w)), const((1, w)),
            const((w, d)), const((d, d)), const((1, d)), const((N_EXPERTS, d)), const((N_EXPERTS, 1)),
            const((TSL, TSL)),
        ],
        out_specs=[tok, tok, rt, rt, const((N_EXPERTS, LANES))],
        out_shape=[
            jax.ShapeDtypeStruct((b, s, d), F32),
            jax.ShapeDtypeStruct((b, s, d), BF16),
            jax.ShapeDtypeStruct((SUBLANES, b * s), jnp.int32),
            jax.ShapeDtypeStruct((SUBLANES, b * s), F32),
            jax.ShapeDtypeStruct((N_EXPERTS, LANES), F32),
        ],
        scratch_shapes=[
            pltpu.VMEM((TSL + SUBLANES, w), F32),
            pltpu.VMEM(scan_shape, F32), pltpu.VMEM(scan_shape, F32),
            pltpu.VMEM(scan_shape, F32), pltpu.VMEM(scan_shape, F32),
            pltpu.VMEM((1, w), F32), pltpu.VMEM((N_EXPERTS, LANES), F32),
        ],
        compiler_params=pltpu.CompilerParams(
            dimension_semantics=("arbitrary", "arbitrary"), vmem_limit_bytes=VMEM_LIMIT),
        name="lru_router",
    )(x, ya, nw, wb, cw, cb, wra, bra, wrx, brx, lam, wpb, wout, nffn, wrt, brt, ustr)


def _invert_slots(slot_flat, n_rows, t):
    n = slot_flat.shape[0]
    vals = jnp.broadcast_to((jnp.arange(n, dtype=jnp.int32) % t)[:, None], (n, SC_ROW))
    mesh = plsc.VectorSubcoreMesh(core_axis_name="core", subcore_axis_name="subcore")

    @functools.partial(pl.kernel, out_type=jax.ShapeDtypeStruct((n_rows, SC_ROW), jnp.int32), mesh=mesh,
                       scratch_types=[])
    def scatter_kernel(x_hbm, i_hbm, o_hbm):
        def body(x_vmem, i_vmem):
            pltpu.sync_copy(x_vmem, o_hbm.at[i_vmem.at[0]])

        pltpu.emit_pipeline(
            body,
            grid=(n // SC_WINDOW,),
            in_specs=[pl.BlockSpec((SC_WINDOW, SC_ROW), lambda i: (i, 0)),
                      pl.BlockSpec((1, SC_WINDOW), lambda i: (0, i))],
            out_specs=[],
            core_axis_name=("core", "subcore"),
            dimension_semantics=(pltpu.PARALLEL,),
        )(x_hbm, i_hbm)

    return scatter_kernel(vals, slot_flat.reshape(1, n))[:, 0]


def _moe_kernel(te_ref, nu_ref, xs_ref, wgu_ref, bg_ref, bl_ref, wd_ref, bd_ref, perm_ref, y_ref,
                wg_s, wl_s, wd_s):
    i = pl.program_id(0)
    half = MXU_COLS // 2
    new_expert = (i == 0) | (te_ref[i] != te_ref[jnp.maximum(i - 1, 0)])

    @pl.when(new_expert & (i < nu_ref[0]))
    def _():
        for blk in range(2 * D_FF // MXU_COLS):
            wblk = wgu_ref[0, :, blk * MXU_COLS:(blk + 1) * MXU_COLS].astype(BF16)
            pw = jnp.dot(wblk, perm_ref[...], preferred_element_type=F32)
            wg_s[:, blk * half:(blk + 1) * half] = pw[:, :half].astype(BF16)
            wl_s[:, blk * half:(blk + 1) * half] = pw[:, half:].astype(BF16)
        wd_s[...] = wd_ref[0].astype(BF16)

    @pl.when(i < nu_ref[0])
    def _():
        xs = xs_ref[...]
        g = jnp.dot(xs, wg_s[...], preferred_element_type=F32) + bg_ref[0]
        l = jnp.dot(xs, wl_s[...], preferred_element_type=F32) + bl_ref[0]
        g = jnp.minimum(g, SWIGLU_LIMIT)
        l = jnp.clip(l, -SWIGLU_LIMIT, SWIGLU_LIMIT)
        act = g * jax.nn.sigmoid(SWIGLU_ALPHA * g) * (l + 1.0)
        y = jnp.dot(act.astype(BF16), wd_s[...], preferred_element_type=F32) + bd_ref[0]
        y_ref[...] = y.astype(y_ref.dtype)

    @pl.when(i >= nu_ref[0])
    def _():
        y_ref[...] = jnp.zeros_like(y_ref)


def _moe_call(tile_expert, n_used, xs, wgu, bg, bl, wd, bd):
    p, d = xs.shape
    n_tiles = p // TM
    half = MXU_COLS // 2
    r = lax.broadcasted_iota(jnp.int32, (MXU_COLS, MXU_COLS), 0)
    c = lax.broadcasted_iota(jnp.int32, (MXU_COLS, MXU_COLS), 1)
    perm = (r == jnp.where(c < half, 2 * c, 2 * (c - half) + 1)).astype(BF16)
    row = lambda i, te, nu: (jnp.minimum(i, nu[0] - 1), 0)
    ex3 = lambda i, te, nu: (te[i], 0, 0)
    grid_spec = pltpu.PrefetchScalarGridSpec(
        num_scalar_prefetch=2,
        grid=(n_tiles,),
        in_specs=[
            pl.BlockSpec((TM, d), row),
            pl.BlockSpec((1, d, 2 * D_FF), ex3),
            pl.BlockSpec((1, 1, D_FF), ex3), pl.BlockSpec((1, 1, D_FF), ex3),
            pl.BlockSpec((1, D_FF, d), ex3), pl.BlockSpec((1, 1, d), ex3),
            pl.BlockSpec((MXU_COLS, MXU_COLS), lambda i, te, nu: (0, 0)),
        ],
        out_specs=pl.BlockSpec((TM, d), lambda i, te, nu: (i, 0)),
        scratch_shapes=[pltpu.VMEM((d, D_FF), BF16), pltpu.VMEM((d, D_FF), BF16), pltpu.VMEM((D_FF, d), BF16)],
    )
    return pl.pallas_call(
        _moe_kernel,
        grid_spec=grid_spec,
        out_shape=jax.ShapeDtypeStruct((p, d), BF16),
        compiler_params=pltpu.CompilerParams(
            dimension_semantics=("arbitrary",), vmem_limit_bytes=VMEM_LIMIT),
        name="moe_experts",
    )(tile_expert, n_used, xs, wgu, bg, bl, wd, bd, perm)


def _final_kernel(x1_ref, yg_ref, gw_ref, nw_ref, out_ref):
    gw = gw_ref[...]
    x2 = x1_ref[...]
    for k in range(TOP_K):
        x2 = x2 + gw[:, k:k + 1] * yg_ref[k].astype(F32)
    out_ref[...] = _rms(x2, nw_ref[...])


def _final_call(x1, yg, gw, nw):
    t, d = x1.shape
    blk = pl.BlockSpec((TF, d), lambda i: (i, 0))
    return pl.pallas_call(
        _final_kernel,
        grid=(t // TF,),
        in_specs=[blk, pl.BlockSpec((TOP_K, TF, d), lambda i: (0, i, 0)),
                  pl.BlockSpec((TF, TOP_K), lambda i: (i, 0)), pl.BlockSpec((1, d), lambda i: (0, 0))],
        out_specs=blk,
        out_shape=jax.ShapeDtypeStruct((t, d), F32),
        compiler_params=pltpu.CompilerParams(dimension_semantics=("arbitrary",)),
        name="final_norm",
    )(x1, yg, gw, nw)


def _chunk_masks():
    r = lax.broadcasted_iota(jnp.int32, (TSH, TSH), 0)
    c = lax.broadcasted_iota(jnp.int32, (TSH, TSH), 1)
    tri = ((r // CHUNK == c // CHUNK) & (r >= c)).astype(BF16)
    r = lax.broadcasted_iota(jnp.int32, (TSL, TSL), 0)
    c = lax.broadcasted_iota(jnp.int32, (TSL, TSL), 1)
    ustr = (r < c).astype(BF16)
    return tri, ustr


def kernel(x, norm_mix, w_in, lb_raw, gn_w, conv_w, conv_b, w_rg_a, b_rg_a, w_rg_x, b_rg_x, lru_lambda,
           w_proj_a, w_proj_b, w_out, norm_ffn, w_router, b_router, w_gu, b_gu, w_down, b_down, norm_final):
    b, s, d = x.shape
    t = b * s
    assert w_in.shape[0] == 1 and s % TSH == 0 and s % TSL == 0 and t % TF == 0
    hk = HA_HEADS * HA_DK
    w = LRU_WIDTH
    tri, ustr = _chunk_masks()

    wi = w_in[0]
    o_lru = 4 * hk
    wa = wi[:, 0:o_lru].astype(BF16)
    wb = jnp.concatenate([wi[:, o_lru:o_lru + 2 * w], wi[:, o_lru + 2 * w + d:o_lru + 2 * w + 2 * d]],
                         axis=1).astype(BF16)
    wga = wi[:, o_lru + 2 * w:o_lru + 2 * w + d].astype(BF16)
    row = lambda v: v.reshape(1, -1)

    ya = _hgrn_call(x, row(norm_mix[0]), wa, wga, lb_raw, row(gn_w[0]), w_proj_a[0].astype(BF16), tri)
    x1, h2, ri, rw, cnt = _lru_call(
        x, ya, row(norm_mix[0]), wb, conv_w[0], row(conv_b[0]), w_rg_a[0].astype(BF16), row(b_rg_a[0]),
        w_rg_x[0].astype(BF16), row(b_rg_x[0]), row(lru_lambda[0]), w_proj_b[0].astype(BF16),
        w_out[0].astype(BF16), row(norm_ffn[0]), w_router[0].T, b_router[0].reshape(-1, 1), ustr)

    idx, pos, gate_w = ri[0:TOP_K], ri[TOP_K:2 * TOP_K], rw[0:TOP_K]
    counts = cnt[:, 0].astype(jnp.int32)
    padded = ((counts + TM - 1) // TM) * TM
    ends = jnp.cumsum(padded)
    offs = ends - padded
    slot = pos
    for e in range(N_EXPERTS):
        slot = slot + jnp.where(idx == e, offs[e], 0)
    n_rows = TOP_K * t + N_EXPERTS * TM
    n_tiles = n_rows // TM
    tile_expert = jnp.minimum(
        jnp.sum(jnp.arange(n_tiles, dtype=jnp.int32)[:, None] * TM >= ends[None, :], axis=1),
        N_EXPERTS - 1).astype(jnp.int32)
    n_used = (ends[-1:] // TM).astype(jnp.int32)

    tok_of_slot = jnp.clip(_invert_slots(slot.reshape(-1), n_rows, t), 0, t - 1)
    xs = h2.reshape(t, d).at[tok_of_slot].get(mode="promise_in_bounds")

    y = _moe_call(
        tile_expert, n_used, xs, w_gu[0],
        b_gu[0][:, 0::2].reshape(N_EXPERTS, 1, D_FF), b_gu[0][:, 1::2].reshape(N_EXPERTS, 1, D_FF),
        w_down[0], b_down[0].reshape(N_EXPERTS, 1, d))

    yg = y.at[slot].get(mode="promise_in_bounds")
    out = _final_call(x1.reshape(t, d), yg, gate_w.T, row(norm_final))
    return out.reshape(b, s, d)
```

```python
import functools

import jax
import jax.numpy as jnp
from jax import lax
from jax.experimental import pallas as pl
from jax.experimental.pallas import tpu as pltpu
from jax.experimental.pallas import tpu_sc as plsc

F32 = jnp.float32
BF16 = jnp.bfloat16

D_MODEL = 1024
HA_HEADS = 8
HA_DK = 128
HA_DV = 128
CHUNK = 64
LRU_WIDTH = 1024
LRU_HEADS = 4
LRU_BLOCK = LRU_WIDTH // LRU_HEADS
CONV_WIDTH = 4
RG_C = 8.0
N_EXPERTS = 32
TOP_K = 4
D_FF = 1024
SWIGLU_LIMIT = 7.0
SWIGLU_ALPHA = 1.702
EPS = 1e-5

SUBLANES = 8
LANES = 128
MXU_COLS = 256
TSH = 256
TSL = 512
TM = 512
TF = 512
VMEM_LIMIT = 56 * 1024 * 1024
SC_WINDOW = 128
SC_ROW = 128

NT_DIMS = (((1,), (1,)), ((), ()))
TN_DIMS = (((0,), (0,)), ((), ()))


def _rms(x, w):
    return x * lax.rsqrt(jnp.mean(x * x, axis=-1, keepdims=True) + EPS) * w


def _split3(v):
    hi = v.astype(BF16)
    r1 = v - hi.astype(F32)
    mid = r1.astype(BF16)
    lo = (r1 - mid.astype(F32)).astype(BF16)
    return hi, mid, lo


def _hgrn_kernel(x_ref, nw_ref, wa_ref, wga_ref, lbraw_ref, gnw_ref, wpa_ref, tri_ref, out_ref,
                 st_ref, qin_ref, kin_ref, kdec_ref, v_ref, g_ref, oa_ref):
    n_chunks = TSH // CHUNK
    hk = HA_HEADS * HA_DK

    @pl.when(pl.program_id(1) == 0)
    def _():
        st_ref[...] = jnp.zeros_like(st_ref)

    hb = _rms(x_ref[0], nw_ref[...]).astype(BF16)

    lbr = lbraw_ref[...]
    rows = [lbr[j:j + 1, :] for j in range(lbr.shape[0])]
    mx = functools.reduce(jnp.maximum, rows)
    es = [jnp.exp(r - mx) for r in rows]
    lb = es[0] / functools.reduce(lambda a, b: a + b, es)

    fl = jnp.dot(hb, wa_ref[:, hk:2 * hk], preferred_element_type=F32)
    f = lb + (1.0 - lb) * jax.nn.sigmoid(fl)
    lf = jnp.log(f)
    k = 1.0 - f
    tri = tri_ref[...]
    bcum = functools.reduce(lambda a, b: a + b,
                            [jnp.dot(tri, p, preferred_element_type=F32) for p in _split3(lf)])
    b_last = [bcum[c * CHUNK + CHUNK - 1:c * CHUNK + CHUNK, :] for c in range(n_chunks)]
    bl_full = jnp.concatenate([jnp.broadcast_to(b, (CHUNK, hk)) for b in b_last], axis=0)

    q = jnp.dot(hb, wa_ref[:, 0:hk], preferred_element_type=F32)
    qin_ref[...] = (jax.nn.silu(q) * jnp.exp(bcum)).astype(BF16)
    kin_ref[...] = (k * jnp.exp(-bcum)).astype(BF16)
    kdec_ref[...] = (k * jnp.exp(bl_full - bcum)).astype(BF16)
    v_ref[...] = jnp.dot(hb, wa_ref[:, 2 * hk:3 * hk], preferred_element_type=F32).astype(BF16)
    g = jnp.dot(hb, wa_ref[:, 3 * hk:4 * hk], preferred_element_type=F32)
    g_ref[...] = jax.nn.silu(g) * gnw_ref[...]

    causal = (lax.broadcasted_iota(jnp.int32, (CHUNK, CHUNK), 0)
              >= lax.broadcasted_iota(jnp.int32, (CHUNK, CHUNK), 1))
    sts = [st_ref[h] for h in range(HA_HEADS)]
    for c in range(n_chunks):
        r = slice(c * CHUNK, (c + 1) * CHUNK)
        edec = jnp.exp(b_last[c])
        for h in range(HA_HEADS):
            l = slice(h * HA_DK, (h + 1) * HA_DK)
            qh, kh, kd, vh = qin_ref[r, l], kin_ref[r, l], kdec_ref[r, l], v_ref[r, l]
            s = lax.dot_general(qh, kh, NT_DIMS, preferred_element_type=F32)
            s = jnp.where(causal, s, 0.0).astype(BF16)
            o = (jnp.dot(s, vh, preferred_element_type=F32)
                 + lax.dot_general(qh, sts[h].astype(BF16), NT_DIMS, preferred_element_type=F32))
            sts[h] = sts[h] * edec[:, l] + lax.dot_general(vh, kd, TN_DIMS, preferred_element_type=F32)
            ms = jnp.mean(o * o, axis=-1, keepdims=True)
            oa_ref[r, l] = (o * lax.rsqrt(ms + EPS) * g_ref[r, l]).astype(BF16)
    for h in range(HA_HEADS):
        st_ref[h] = sts[h]

    ya = jnp.dot(oa_ref[...], wpa_ref[...], preferred_element_type=F32)
    ga = jnp.dot(hb, wga_ref[...], preferred_element_type=F32)
    out_ref[0] = jax.nn.sigmoid(ga) * ya


def _hgrn_call(x, nw, wa, wga, lb_raw, gnw, wpa, tri):
    b, s, d = x.shape
    hk = HA_HEADS * HA_DK
    const = lambda shape: pl.BlockSpec(shape, lambda i, j: (0,) * len(shape))
    return pl.pallas_call(
        _hgrn_kernel,
        grid=(b, s // TSH),
        in_specs=[
            pl.BlockSpec((1, TSH, d), lambda i, j: (i, j, 0)),
            const((1, d)), const((d, 4 * hk)), const((d, d)), const(lb_raw.shape), const((1, hk)),
            const((hk, d)), const((TSH, TSH)),
        ],
        out_specs=pl.BlockSpec((1, TSH, d), lambda i, j: (i, j, 0)),
        out_shape=jax.ShapeDtypeStruct((b, s, d), F32),
        scratch_shapes=[
            pltpu.VMEM((HA_HEADS, HA_DV, HA_DK), F32),
            pltpu.VMEM((TSH, hk), BF16), pltpu.VMEM((TSH, hk), BF16), pltpu.VMEM((TSH, hk), BF16),
            pltpu.VMEM((TSH, hk), BF16), pltpu.VMEM((TSH, hk), F32), pltpu.VMEM((TSH, hk), BF16),
        ],
        compiler_params=pltpu.CompilerParams(
            dimension_semantics=("arbitrary", "arbitrary"), vmem_limit_bytes=VMEM_LIMIT),
        name="hgrn",
    )(x, nw, wa, wga, lb_raw, gnw, wpa, tri)


def _lru_kernel(x_ref, ya_ref, nw_ref, wb_ref, cw_ref, cb_ref, wra_ref, bra_ref, wrx_ref, brx_ref, lam_ref,
                wpb_ref, wout_ref, nffn_ref, wrt_ref, brt_ref, ustr_ref,
                x1_ref, h2_ref, ri_ref, rw_ref, cnt_ref,
                xbuf_ref, a_ref, u_ref, hl_ref, pp_ref, hc_ref, carry_ref):
    seg = TSL // SUBLANES
    pitch = seg + SUBLANES
    w = LRU_WIDTH
    bi, si = pl.program_id(0), pl.program_id(1)

    @pl.when(si == 0)
    def _():
        xbuf_ref[0:SUBLANES, :] = jnp.zeros((SUBLANES, w), F32)
        hc_ref[...] = jnp.zeros_like(hc_ref)

    @pl.when((bi == 0) & (si == 0))
    def _():
        carry_ref[...] = jnp.zeros_like(carry_ref)

    x = x_ref[0]
    hb = _rms(x, nw_ref[...]).astype(BF16)

    xb = jnp.dot(hb, wb_ref[:, 0:w], preferred_element_type=F32)
    xbuf_ref[SUBLANES:SUBLANES + TSL, :] = xb
    cw = cw_ref[...]
    xc = cb_ref[...] + cw[CONV_WIDTH - 1:CONV_WIDTH, :] * xb
    for kk in range(CONV_WIDTH - 1):
        shift = CONV_WIDTH - 1 - kk
        xc = xc + cw[kk:kk + 1, :] * xbuf_ref[SUBLANES - shift:SUBLANES - shift + TSL, :]
    xbuf_ref[0:SUBLANES, :] = xbuf_ref[TSL:TSL + SUBLANES, :]

    xcb = xc.astype(BF16)
    r_parts, i_parts = [], []
    for j in range(LRU_HEADS):
        l = slice(j * LRU_BLOCK, (j + 1) * LRU_BLOCK)
        r_parts.append(jnp.dot(xcb[:, l], wra_ref[j], preferred_element_type=F32))
        i_parts.append(jnp.dot(xcb[:, l], wrx_ref[j], preferred_element_type=F32))
    rg = jax.nn.sigmoid(jnp.concatenate(r_parts, axis=1) + bra_ref[...])
    ig = jax.nn.sigmoid(jnp.concatenate(i_parts, axis=1) + brx_ref[...])
    log_a = RG_C * rg * jax.nn.log_sigmoid(lam_ref[...])
    a = jnp.exp(log_a)
    mult = jnp.sqrt(1.0 - a * a)
    is_first = (lax.broadcasted_iota(jnp.int32, (TSL, 1), 0) == 0) & (si == 0)
    mult = jnp.where(is_first, 1.0, mult)
    u = xc * ig * mult
    n_groups = w // LANES
    group_rows = SUBLANES * pitch
    for gi in range(n_groups):
        for j in range(SUBLANES):
            dst = slice(gi * group_rows + j * pitch, gi * group_rows + j * pitch + seg)
            a_ref[dst, :] = a[j * seg:(j + 1) * seg, gi * LANES:(gi + 1) * LANES]
            u_ref[dst, :] = u[j * seg:(j + 1) * seg, gi * LANES:(gi + 1) * LANES]

    def step(i, hp):
        hs, ps = hp
        new_h, new_p = [], []
        for gi in range(n_groups):
            rows = pl.ds(gi * group_rows + i, SUBLANES, stride=pitch)
            ai = a_ref[rows, :]
            h = ai * hs[gi] + u_ref[rows, :]
            p = ai * ps[gi]
            hl_ref[rows, :] = h
            pp_ref[rows, :] = p
            new_h.append(h)
            new_p.append(p)
        return tuple(new_h), tuple(new_p)

    init = (tuple(jnp.zeros((SUBLANES, LANES), F32) for _ in range(n_groups)),
            tuple(jnp.ones((SUBLANES, LANES), F32) for _ in range(n_groups)))
    h_end, p_end = lax.fori_loop(0, seg, step, init, unroll=4)
    h_end = jnp.concatenate(h_end, axis=1)
    p_end = jnp.concatenate(p_end, axis=1)
    c = hc_ref[...]
    h_parts = []
    for j in range(SUBLANES):
        src = [slice(gi * group_rows + j * pitch, gi * group_rows + j * pitch + seg) for gi in range(n_groups)]
        hl = jnp.concatenate([hl_ref[r, :] for r in src], axis=1)
        pp = jnp.concatenate([pp_ref[r, :] for r in src], axis=1)
        h_parts.append(hl + pp * c)
        c = h_end[j:j + 1, :] + p_end[j:j + 1, :] * c
    hc_ref[...] = c
    h_full = jnp.concatenate(h_parts, axis=0)

    gate_b = jnp.dot(hb, wb_ref[:, w:2 * w], preferred_element_type=F32)
    ob = (h_full * jax.nn.gelu(gate_b)).astype(BF16)
    yb = jnp.dot(ob, wpb_ref[...], preferred_element_type=F32)
    gb = jnp.dot(hb, wb_ref[:, 2 * w:3 * w], preferred_element_type=F32)
    merged = ya_ref[0] + jax.nn.sigmoid(gb) * yb
    x1 = x + jnp.dot(merged.astype(BF16), wout_ref[...], preferred_element_type=F32)
    x1_ref[0] = x1

    h2 = _rms(x1, nffn_ref[...])
    h_hi = h2.astype(BF16)
    h2_ref[0] = h_hi
    h_lo = (h2 - h_hi.astype(F32)).astype(BF16)
    wt = wrt_ref[...]
    w_hi = wt.astype(BF16)
    w_lo = (wt - w_hi.astype(F32)).astype(BF16)
    logits = (lax.dot_general(w_hi, h_hi, NT_DIMS, preferred_element_type=F32)
              + lax.dot_general(w_hi, h_lo, NT_DIMS, preferred_element_type=F32)
              + lax.dot_general(w_lo, h_hi, NT_DIMS, preferred_element_type=F32)) + brt_ref[...]

    eio = lax.broadcasted_iota(jnp.int32, (N_EXPERTS, TSL), 0)
    idxs, vals = [], []
    for _ in range(TOP_K):
        m = jnp.max(logits, axis=0, keepdims=True)
        ik = jnp.min(jnp.where(logits == m, eio, N_EXPERTS), axis=0, keepdims=True)
        idxs.append(ik)
        vals.append(m)
        logits = jnp.where(eio == ik, -jnp.inf, logits)
    exps = [jnp.exp(v - vals[0]) for v in vals]
    den = functools.reduce(lambda p, q: p + q, exps)
    onehots = [(eio == ik) for ik in idxs]
    oh = functools.reduce(lambda p, q: p + q, [o.astype(F32) for o in onehots])
    cum = jnp.dot(oh.astype(BF16), ustr_ref[...], preferred_element_type=F32) + carry_ref[:, 0:1]
    poss = [jnp.sum(jnp.where(o, cum, 0.0), axis=0, keepdims=True).astype(jnp.int32) for o in onehots]
    carry_ref[...] = carry_ref[...] + jnp.sum(oh, axis=1, keepdims=True)
    ri_ref[...] = jnp.concatenate(idxs + poss, axis=0)
    rw_ref[...] = jnp.concatenate([e / den for e in exps] + [jnp.zeros((SUBLANES - TOP_K, TSL), F32)], axis=0)
    cnt_ref[...] = carry_ref[...]


def _lru_call(x, ya, nw, wb, cw, cb, wra, bra, wrx, brx, lam, wpb, wout, nffn, wrt, brt, ustr):
    b, s, d = x.shape
    w = LRU_WIDTH
    scan_shape = ((TSL + SUBLANES * SUBLANES) * (w // LANES), LANES)
    const = lambda shape: pl.BlockSpec(shape, lambda i, j: (0,) * len(shape))
    tok = pl.BlockSpec((1, TSL, d), lambda i, j: (i, j, 0))
    rt = pl.BlockSpec((SUBLANES, TSL), lambda i, j: (0, i * (s // TSL) + j))
    return pl.pallas_call(
        _lru_kernel,
        grid=(b, s // TSL),
        in_specs=[
            tok, tok, const((1, d)), const((d, 3 * w)), const((CONV_WIDTH, w)), const((1, w)),
            const((LRU_HEADS, LRU_BLOCK, LRU_BLOCK)), const((1, w)),
            const((LRU_HEADS, LRU_BLOCK, LRU_BLOCK)), const((1, w)), const((1, w)),
            const((w, d)), const((d, d)), const((1, d)), const((N_EXPERTS, d)), const((N_EXPERTS, 1)),
            const((TSL, TSL)),
        ],
        out_specs=[tok, tok, rt, rt, const((N_EXPERTS, LANES))],
        out_shape=[
            jax.ShapeDtypeStruct((b, s, d), F32),
            jax.ShapeDtypeStruct((b, s, d), BF16),
            jax.ShapeDtypeStruct((SUBLANES, b * s), jnp.int32),
            jax.ShapeDtypeStruct((SUBLANES, b * s), F32),
            jax.ShapeDtypeStruct((N_EXPERTS, LANES), F32),
        ],
        scratch_shapes=[
            pltpu.VMEM((TSL + SUBLANES, w), F32),
            pltpu.VMEM(scan_shape, F32), pltpu.VMEM(scan_shape, F32),
            pltpu.VMEM(scan_shape, F32), pltpu.VMEM(scan_shape, F32),
            pltpu.VMEM((1, w), F32), pltpu.VMEM((N_EXPERTS, LANES), F32),
        ],
        compiler_params=pltpu.CompilerParams(
            dimension_semantics=("arbitrary", "arbitrary"), vmem_limit_bytes=VMEM_LIMIT),
        name="lru_router",
    )(x, ya, nw, wb, cw, cb, wra, bra, wrx, brx, lam, wpb, wout, nffn, wrt, brt, ustr)


def _invert_slots(slot_flat, n_rows, t):
    n = slot_flat.shape[0]
    vals = jnp.broadcast_to((jnp.arange(n, dtype=jnp.int32) % t)[:, None], (n, SC_ROW))
    mesh = plsc.VectorSubcoreMesh(core_axis_name="core", subcore_axis_name="subcore")

    @functools.partial(pl.kernel, out_type=jax.ShapeDtypeStruct((n_rows, SC_ROW), jnp.int32), mesh=mesh,
                       scratch_types=[])
    def scatter_kernel(x_hbm, i_hbm, o_hbm):
        def body(x_vmem, i_vmem):
            pltpu.sync_copy(x_vmem, o_hbm.at[i_vmem.at[0]])

        pltpu.emit_pipeline(
            body,
            grid=(n // SC_WINDOW,),
            in_specs=[pl.BlockSpec((SC_WINDOW, SC_ROW), lambda i: (i, 0)),
                      pl.BlockSpec((1, SC_WINDOW), lambda i: (0, i))],
            out_specs=[],
            core_axis_name=("core", "subcore"),
            dimension_semantics=(pltpu.PARALLEL,),
        )(x_hbm, i_hbm)

    return scatter_kernel(vals, slot_flat.reshape(1, n))[:, 0]


def _moe_kernel(te_ref, nu_ref, xs_ref, wgu_ref, bg_ref, bl_ref, wd_ref, bd_ref, perm_ref, y_ref,
                wg_s, wl_s, wd_s):
    i = pl.program_id(0)
    half = MXU_COLS // 2
    new_expert = (i == 0) | (te_ref[i] != te_ref[jnp.maximum(i - 1, 0)])

    @pl.when(new_expert & (i < nu_ref[0]))
    def _():
        for blk in range(2 * D_FF // MXU_COLS):
            wblk = wgu_ref[0, :, blk * MXU_COLS:(blk + 1) * MXU_COLS].astype(BF16)
            pw = jnp.dot(wblk, perm_ref[...], preferred_element_type=F32)
            wg_s[:, blk * half:(blk + 1) * half] = pw[:, :half].astype(BF16)
            wl_s[:, blk * half:(blk + 1) * half] = pw[:, half:].astype(BF16)
        wd_s[...] = wd_ref[0].astype(BF16)

    @pl.when(i < nu_ref[0])
    def _():
        xs = xs_ref[...]
        g = jnp.dot(xs, wg_s[...], preferred_element_type=F32) + bg_ref[0]
        l = jnp.dot(xs, wl_s[...], preferred_element_type=F32) + bl_ref[0]
        g = jnp.minimum(g, SWIGLU_LIMIT)
        l = jnp.clip(l, -SWIGLU_LIMIT, SWIGLU_LIMIT)
        act = g * jax.nn.sigmoid(SWIGLU_ALPHA * g) * (l + 1.0)
        y = jnp.dot(act.astype(BF16), wd_s[...], preferred_element_type=F32) + bd_ref[0]
        y_ref[...] = y.astype(y_ref.dtype)

    @pl.when(i >= nu_ref[0])
    def _():
        y_ref[...] = jnp.zeros_like(y_ref)


def _moe_call(tile_expert, n_used, xs, wgu, bg, bl, wd, bd):
    p, d = xs.shape
    n_tiles = p // TM
    half = MXU_COLS // 2
    r = lax.broadcasted_iota(jnp.int32, (MXU_COLS, MXU_COLS), 0)
    c = lax.broadcasted_iota(jnp.int32, (MXU_COLS, MXU_COLS), 1)
    perm = (r == jnp.where(c < half, 2 * c, 2 * (c - half) + 1)).astype(BF16)
    row = lambda i, te, nu: (jnp.minimum(i, nu[0] - 1), 0)
    ex3 = lambda i, te, nu: (te[i], 0, 0)
    grid_spec = pltpu.PrefetchScalarGridSpec(
        num_scalar_prefetch=2,
        grid=(n_tiles,),
        in_specs=[
            pl.BlockSpec((TM, d), row),
            pl.BlockSpec((1, d, 2 * D_FF), ex3),
            pl.BlockSpec((1, 1, D_FF), ex3), pl.BlockSpec((1, 1, D_FF), ex3),
            pl.BlockSpec((1, D_FF, d), ex3), pl.BlockSpec((1, 1, d), ex3),
            pl.BlockSpec((MXU_COLS, MXU_COLS), lambda i, te, nu: (0, 0)),
        ],
        out_specs=pl.BlockSpec((TM, d), lambda i, te, nu: (i, 0)),
        scratch_shapes=[pltpu.VMEM((d, D_FF), BF16), pltpu.VMEM((d, D_FF), BF16), pltpu.VMEM((D_FF, d), BF16)],
    )
    return pl.pallas_call(
        _moe_kernel,
        grid_spec=grid_spec,
        out_shape=jax.ShapeDtypeStruct((p, d), BF16),
        compiler_params=pltpu.CompilerParams(
            dimension_semantics=("arbitrary",), vmem_limit_bytes=VMEM_LIMIT),
        name="moe_experts",
    )(tile_expert, n_used, xs, wgu, bg, bl, wd, bd, perm)


def _final_kernel(x1_ref, yg_ref, gw_ref, nw_ref, out_ref):
    gw = gw_ref[...]
    x2 = x1_ref[...]
    for k in range(TOP_K):
        x2 = x2 + gw[:, k:k + 1] * yg_ref[k].astype(F32)
    out_ref[...] = _rms(x2, nw_ref[...])


def _final_call(x1, yg, gw, nw):
    t, d = x1.shape
    blk = pl.BlockSpec((TF, d), lambda i: (i, 0))
    return pl.pallas_call(
        _final_kernel,
        grid=(t // TF,),
        in_specs=[blk, pl.BlockSpec((TOP_K, TF, d), lambda i: (0, i, 0)),
                  pl.BlockSpec((TF, TOP_K), lambda i: (i, 0)), pl.BlockSpec((1, d), lambda i: (0, 0))],
        out_specs=blk,
        out_shape=jax.ShapeDtypeStruct((t, d), F32),
        compiler_params=pltpu.CompilerParams(dimension_semantics=("arbitrary",)),
        name="final_norm",
    )(x1, yg, gw, nw)


def _chunk_masks():
    r = lax.broadcasted_iota(jnp.int32, (TSH, TSH), 0)
    c = lax.broadcasted_iota(jnp.int32, (TSH, TSH), 1)
    tri = ((r // CHUNK == c // CHUNK) & (r >= c)).astype(BF16)
    r = lax.broadcasted_iota(jnp.int32, (TSL, TSL), 0)
    c = lax.broadcasted_iota(jnp.int32, (TSL, TSL), 1)
    ustr = (r < c).astype(BF16)
    return tri, ustr


def kernel(x, norm_mix, w_in, lb_raw, gn_w, conv_w, conv_b, w_rg_a, b_rg_a, w_rg_x, b_rg_x, lru_lambda,
           w_proj_a, w_proj_b, w_out, norm_ffn, w_router, b_router, w_gu, b_gu, w_down, b_down, norm_final):
    b, s, d = x.shape
    t = b * s
    assert w_in.shape[0] == 1 and s % TSH == 0 and s % TSL == 0 and t % TF == 0
    hk = HA_HEADS * HA_DK
    w = LRU_WIDTH
    tri, ustr = _chunk_masks()

    wi = w_in[0]
    o_lru = 4 * hk
    wa = wi[:, 0:o_lru].astype(BF16)
    wb = jnp.concatenate([wi[:, o_lru:o_lru + 2 * w], wi[:, o_lru + 2 * w + d:o_lru + 2 * w + 2 * d]],
                         axis=1).astype(BF16)
    wga = wi[:, o_lru + 2 * w:o_lru + 2 * w + d].astype(BF16)
    row = lambda v: v.reshape(1, -1)

    ya = _hgrn_call(x, row(norm_mix[0]), wa, wga, lb_raw, row(gn_w[0]), w_proj_a[0].astype(BF16), tri)
    x1, h2, ri, rw, cnt = _lru_call(
        x, ya, row(norm_mix[0]), wb, conv_w[0], row(conv_b[0]), w_rg_a[0].astype(BF16), row(b_rg_a[0]),
        w_rg_x[0].astype(BF16), row(b_rg_x[0]), row(lru_lambda[0]), w_proj_b[0].astype(BF16),
        w_out[0].astype(BF16), row(norm_ffn[0]), w_router[0].T, b_router[0].reshape(-1, 1), ustr)

    idx, pos, gate_w = ri[0:TOP_K], ri[TOP_K:2 * TOP_K], rw[0:TOP_K]
    counts = cnt[:, 0].astype(jnp.int32)
    padded = ((counts + TM - 1) // TM) * TM
    ends = jnp.cumsum(padded)
    offs = ends - padded
    slot = pos
    for e in range(N_EXPERTS):
        slot = slot + jnp.where(idx == e, offs[e], 0)
    n_rows = TOP_K * t + N_EXPERTS * TM
    n_tiles = n_rows // TM
    tile_expert = jnp.minimum(
        jnp.sum(jnp.arange(n_tiles, dtype=jnp.int32)[:, None] * TM >= ends[None, :], axis=1),
        N_EXPERTS - 1).astype(jnp.int32)
    n_used = (ends[-1:] // TM).astype(jnp.int32)

    n_pad = n_rows - TOP_K * t
    pad_ends = jnp.cumsum(padded - counts)
    j = jnp.arange(n_pad, dtype=jnp.int32)
    pe = jnp.sum(j[:, None] >= pad_ends[None, :], axis=1)
    pe_c = jnp.minimum(pe, N_EXPERTS - 1)
    in_expert = jnp.take(offs + counts, pe_c) + j - jnp.take(pad_ends - (padded - counts), pe_c)
    pad_slot = jnp.where(pe < N_EXPERTS, in_expert, ends[-1] + j - pad_ends[-1])
    all_slots = jnp.concatenate([slot.reshape(-1), pad_slot.astype(jnp.int32)])
    tok_of_slot = _invert_slots(all_slots, n_rows, t)
    xs = h2.reshape(t, d).at[tok_of_slot].get(mode="promise_in_bounds")

    y = _moe_call(
        tile_expert, n_used, xs, w_gu[0],
        b_gu[0][:, 0::2].reshape(N_EXPERTS, 1, D_FF), b_gu[0][:, 1::2].reshape(N_EXPERTS, 1, D_FF),
        w_down[0], b_down[0].reshape(N_EXPERTS, 1, d))

    yg = y.at[slot].get(mode="promise_in_bounds")
    out = _final_call(x1.reshape(t, d), yg, gate_w.T, row(norm_final))
    return out.reshape(b, s, d)
```

```python
import functools

import jax
import jax.numpy as jnp
from jax import lax
from jax.experimental import pallas as pl
from jax.experimental.pallas import tpu as pltpu
from jax.experimental.pallas import tpu_sc as plsc

F32 = jnp.float32
BF16 = jnp.bfloat16

D_MODEL = 1024
HA_HEADS = 8
HA_DK = 128
HA_DV = 128
CHUNK = 64
LRU_WIDTH = 1024
LRU_HEADS = 4
LRU_BLOCK = LRU_WIDTH // LRU_HEADS
CONV_WIDTH = 4
RG_C = 8.0
N_EXPERTS = 32
TOP_K = 4
D_FF = 1024
SWIGLU_LIMIT = 7.0
SWIGLU_ALPHA = 1.702
EPS = 1e-5

SUBLANES = 8
LANES = 128
MXU_COLS = 256
TSH = 256
TSL = 512
TM = 512
TF = 512
N_PARTS = 2
VMEM_LIMIT = 56 * 1024 * 1024
SC_WINDOW = 128
SC_ROW = 128

NT_DIMS = (((1,), (1,)), ((), ()))
TN_DIMS = (((0,), (0,)), ((), ()))


def _rms(x, w):
    return x * lax.rsqrt(jnp.mean(x * x, axis=-1, keepdims=True) + EPS) * w


def _split3(v):
    hi = v.astype(BF16)
    r1 = v - hi.astype(F32)
    mid = r1.astype(BF16)
    lo = (r1 - mid.astype(F32)).astype(BF16)
    return hi, mid, lo


def _hgrn_kernel(x_ref, nw_ref, wa_ref, wga_ref, lbraw_ref, gnw_ref, wpa_ref, tri_ref, out_ref,
                 st_ref, qin_ref, kin_ref, kdec_ref, v_ref, g_ref, oa_ref):
    n_chunks = TSH // CHUNK
    hk = HA_HEADS * HA_DK

    @pl.when(pl.program_id(1) == 0)
    def _():
        st_ref[...] = jnp.zeros_like(st_ref)

    hb = _rms(x_ref[0], nw_ref[...]).astype(BF16)

    lbr = lbraw_ref[...]
    rows = [lbr[j:j + 1, :] for j in range(lbr.shape[0])]
    mx = functools.reduce(jnp.maximum, rows)
    es = [jnp.exp(r - mx) for r in rows]
    lb = es[0] / functools.reduce(lambda a, b: a + b, es)

    fl = jnp.dot(hb, wa_ref[:, hk:2 * hk], preferred_element_type=F32)
    f = lb + (1.0 - lb) * jax.nn.sigmoid(fl)
    lf = jnp.log(f)
    k = 1.0 - f
    tri = tri_ref[...]
    bcum = functools.reduce(lambda a, b: a + b,
                            [jnp.dot(tri, p, preferred_element_type=F32) for p in _split3(lf)])
    b_last = [bcum[c * CHUNK + CHUNK - 1:c * CHUNK + CHUNK, :] for c in range(n_chunks)]
    bl_full = jnp.concatenate([jnp.broadcast_to(b, (CHUNK, hk)) for b in b_last], axis=0)

    q = jnp.dot(hb, wa_ref[:, 0:hk], preferred_element_type=F32)
    qin_ref[...] = (jax.nn.silu(q) * jnp.exp(bcum)).astype(BF16)
    kin_ref[...] = (k * jnp.exp(-bcum)).astype(BF16)
    kdec_ref[...] = (k * jnp.exp(bl_full - bcum)).astype(BF16)
    v_ref[...] = jnp.dot(hb, wa_ref[:, 2 * hk:3 * hk], preferred_element_type=F32).astype(BF16)
    g = jnp.dot(hb, wa_ref[:, 3 * hk:4 * hk], preferred_element_type=F32)
    g_ref[...] = jax.nn.silu(g) * gnw_ref[...]

    causal = (lax.broadcasted_iota(jnp.int32, (CHUNK, CHUNK), 0)
              >= lax.broadcasted_iota(jnp.int32, (CHUNK, CHUNK), 1))
    sts = [st_ref[h] for h in range(HA_HEADS)]
    for c in range(n_chunks):
        r = slice(c * CHUNK, (c + 1) * CHUNK)
        edec = jnp.exp(b_last[c])
        for h in range(HA_HEADS):
            l = slice(h * HA_DK, (h + 1) * HA_DK)
            qh, kh, kd, vh = qin_ref[r, l], kin_ref[r, l], kdec_ref[r, l], v_ref[r, l]
            s = lax.dot_general(qh, kh, NT_DIMS, preferred_element_type=F32)
            s = jnp.where(causal, s, 0.0).astype(BF16)
            o = (jnp.dot(s, vh, preferred_element_type=F32)
                 + jnp.dot(qh, sts[h].T.astype(BF16), preferred_element_type=F32))
            sts[h] = sts[h] * edec[:, l] + lax.dot_general(vh, kd, TN_DIMS, preferred_element_type=F32)
            ms = jnp.mean(o * o, axis=-1, keepdims=True)
            oa_ref[r, l] = (o * lax.rsqrt(ms + EPS) * g_ref[r, l]).astype(BF16)
    for h in range(HA_HEADS):
        st_ref[h] = sts[h]

    ya = jnp.dot(oa_ref[...], wpa_ref[...], preferred_element_type=F32)
    ga = jnp.dot(hb, wga_ref[...], preferred_element_type=F32)
    out_ref[0] = jax.nn.sigmoid(ga) * ya


def _hgrn_call(x, nw, wa, wga, lb_raw, gnw, wpa, tri):
    b, s, d = x.shape
    hk = HA_HEADS * HA_DK
    const = lambda shape: pl.BlockSpec(shape, lambda i, j: (0,) * len(shape))
    return pl.pallas_call(
        _hgrn_kernel,
        grid=(b, s // TSH),
        in_specs=[
            pl.BlockSpec((1, TSH, d), lambda i, j: (i, j, 0)),
            const((1, d)), const((d, 4 * hk)), const((d, d)), const(lb_raw.shape), const((1, hk)),
            const((hk, d)), const((TSH, TSH)),
        ],
        out_specs=pl.BlockSpec((1, TSH, d), lambda i, j: (i, j, 0)),
        out_shape=jax.ShapeDtypeStruct((b, s, d), F32),
        scratch_shapes=[
            pltpu.VMEM((HA_HEADS, HA_DV, HA_DK), F32),
            pltpu.VMEM((TSH, hk), BF16), pltpu.VMEM((TSH, hk), BF16), pltpu.VMEM((TSH, hk), BF16),
            pltpu.VMEM((TSH, hk), BF16), pltpu.VMEM((TSH, hk), F32), pltpu.VMEM((TSH, hk), BF16),
        ],
        compiler_params=pltpu.CompilerParams(
            dimension_semantics=("arbitrary", "arbitrary"), vmem_limit_bytes=VMEM_LIMIT),
        name="hgrn",
    )(x, nw, wa, wga, lb_raw, gnw, wpa, tri)


def _lru_kernel(x_ref, ya_ref, nw_ref, wb_ref, cw_ref, cb_ref, wra_ref, bra_ref, wrx_ref, brx_ref, lam_ref,
                wpb_ref, wout_ref, nffn_ref, wrt_ref, brt_ref, ustr_ref,
                x1_ref, h2_ref, ri_ref, rw_ref, cnt_ref,
                xbuf_ref, a_ref, u_ref, hl_ref, pp_ref, hc_ref, carry_ref):
    seg = TSL // SUBLANES
    pitch = seg + SUBLANES
    w = LRU_WIDTH
    bi, si = pl.program_id(0), pl.program_id(1)

    @pl.when(si == 0)
    def _():
        xbuf_ref[0:SUBLANES, :] = jnp.zeros((SUBLANES, w), F32)
        hc_ref[...] = jnp.zeros_like(hc_ref)

    @pl.when((bi == 0) & (si == 0))
    def _():
        carry_ref[...] = jnp.zeros_like(carry_ref)

    x = x_ref[0]
    hb = _rms(x, nw_ref[...]).astype(BF16)

    xb = jnp.dot(hb, wb_ref[:, 0:w], preferred_element_type=F32)
    xbuf_ref[SUBLANES:SUBLANES + TSL, :] = xb
    cw = cw_ref[...]
    xc = cb_ref[...] + cw[CONV_WIDTH - 1:CONV_WIDTH, :] * xb
    for kk in range(CONV_WIDTH - 1):
        shift = CONV_WIDTH - 1 - kk
        xc = xc + cw[kk:kk + 1, :] * xbuf_ref[SUBLANES - shift:SUBLANES - shift + TSL, :]
    xbuf_ref[0:SUBLANES, :] = xbuf_ref[TSL:TSL + SUBLANES, :]

    xcb = xc.astype(BF16)
    r_parts, i_parts = [], []
    for j in range(LRU_HEADS):
        l = slice(j * LRU_BLOCK, (j + 1) * LRU_BLOCK)
        r_parts.append(jnp.dot(xcb[:, l], wra_ref[j], preferred_element_type=F32))
        i_parts.append(jnp.dot(xcb[:, l], wrx_ref[j], preferred_element_type=F32))
    rg = jax.nn.sigmoid(jnp.concatenate(r_parts, axis=1) + bra_ref[...])
    ig = jax.nn.sigmoid(jnp.concatenate(i_parts, axis=1) + brx_ref[...])
    log_a = RG_C * rg * jax.nn.log_sigmoid(lam_ref[...])
    a = jnp.exp(log_a)
    m2 = 1.0 - a * a
    mult = jnp.where(m2 > 0.0, m2 * lax.rsqrt(m2), 0.0)
    is_first = (lax.broadcasted_iota(jnp.int32, (TSL, 1), 0) == 0) & (si == 0)
    mult = jnp.where(is_first, 1.0, mult)
    u = xc * ig * mult
    n_groups = w // LANES
    group_rows = SUBLANES * pitch
    for gi in range(n_groups):
        for j in range(SUBLANES):
            dst = slice(gi * group_rows + j * pitch, gi * group_rows + j * pitch + seg)
            a_ref[dst, :] = a[j * seg:(j + 1) * seg, gi * LANES:(gi + 1) * LANES]
            u_ref[dst, :] = u[j * seg:(j + 1) * seg, gi * LANES:(gi + 1) * LANES]

    def step(i, hp):
        hs, ps = hp
        new_h, new_p = [], []
        for gi in range(n_groups):
            rows = pl.ds(gi * group_rows + i, SUBLANES, stride=pitch)
            ai = a_ref[rows, :]
            h = ai * hs[gi] + u_ref[rows, :]
            p = ai * ps[gi]
            hl_ref[rows, :] = h
            pp_ref[rows, :] = p
            new_h.append(h)
            new_p.append(p)
        return tuple(new_h), tuple(new_p)

    init = (tuple(jnp.zeros((SUBLANES, LANES), F32) for _ in range(n_groups)),
            tuple(jnp.ones((SUBLANES, LANES), F32) for _ in range(n_groups)))
    h_end, p_end = lax.fori_loop(0, seg, step, init, unroll=4)
    h_end = jnp.concatenate(h_end, axis=1)
    p_end = jnp.concatenate(p_end, axis=1)
    c = hc_ref[...]
    h_parts = []
    for j in range(SUBLANES):
        src = [slice(gi * group_rows + j * pitch, gi * group_rows + j * pitch + seg) for gi in range(n_groups)]
        hl = jnp.concatenate([hl_ref[r, :] for r in src], axis=1)
        pp = jnp.concatenate([pp_ref[r, :] for r in src], axis=1)
        h_parts.append(hl + pp * c)
        c = h_end[j:j + 1, :] + p_end[j:j + 1, :] * c
    hc_ref[...] = c
    h_full = jnp.concatenate(h_parts, axis=0)

    gate_b = jnp.dot(hb, wb_ref[:, w:2 * w], preferred_element_type=F32)
    ob = (h_full * jax.nn.gelu(gate_b)).astype(BF16)
    yb = jnp.dot(ob, wpb_ref[...], preferred_element_type=F32)
    gb = jnp.dot(hb, wb_ref[:, 2 * w:3 * w], preferred_element_type=F32)
    merged = ya_ref[0] + jax.nn.sigmoid(gb) * yb
    x1 = x + jnp.dot(merged.astype(BF16), wout_ref[...], preferred_element_type=F32)
    x1_ref[0] = x1

    h2 = _rms(x1, nffn_ref[...])
    h_hi = h2.astype(BF16)
    h2_ref[0] = h_hi
    h_lo = (h2 - h_hi.astype(F32)).astype(BF16)
    wt = wrt_ref[...]
    w_hi = wt.astype(BF16)
    w_lo = (wt - w_hi.astype(F32)).astype(BF16)
    both = lax.dot_general(jnp.concatenate([w_hi, w_lo], axis=0), h_hi, NT_DIMS, preferred_element_type=F32)
    logits = (both[:N_EXPERTS] + both[N_EXPERTS:]
              + lax.dot_general(w_hi, h_lo, NT_DIMS, preferred_element_type=F32)) + brt_ref[...]

    eio = lax.broadcasted_iota(jnp.int32, (N_EXPERTS, TSL), 0)
    idxs, vals = [], []
    for _ in range(TOP_K):
        m = jnp.max(logits, axis=0, keepdims=True)
        ik = jnp.min(jnp.where(logits == m, eio, N_EXPERTS), axis=0, keepdims=True)
        idxs.append(ik)
        vals.append(m)
        logits = jnp.where(eio == ik, -jnp.inf, logits)
    exps = [jnp.exp(v - vals[0]) for v in vals]
    den = functools.reduce(lambda p, q: p + q, exps)
    onehots = [(eio == ik) for ik in idxs]
    oh = functools.reduce(lambda p, q: p + q, [o.astype(F32) for o in onehots])
    cum = jnp.dot(oh.astype(BF16), ustr_ref[...], preferred_element_type=F32) + carry_ref[:, 0:1]
    poss = [jnp.sum(jnp.where(o, cum, 0.0), axis=0, keepdims=True).astype(jnp.int32) for o in onehots]
    carry_ref[...] = carry_ref[...] + jnp.sum(oh, axis=1, keepdims=True)
    ri_ref[...] = jnp.concatenate(idxs + poss, axis=0)
    rw_ref[...] = jnp.concatenate([e / den for e in exps] + [jnp.zeros((SUBLANES - TOP_K, TSL), F32)], axis=0)
    cnt_ref[...] = carry_ref[...]


def _lru_call(x, ya, nw, wb, cw, cb, wra, bra, wrx, brx, lam, wpb, wout, nffn, wrt, brt, ustr):
    b, s, d = x.shape
    w = LRU_WIDTH
    scan_shape = ((TSL + SUBLANES * SUBLANES) * (w // LANES), LANES)
    const = lambda shape: pl.BlockSpec(shape, lambda i, j: (0,) * len(shape))
    tok = pl.BlockSpec((1, TSL, d), lambda i, j: (i, j, 0))
    rt = pl.BlockSpec((SUBLANES, TSL), lambda i, j: (0, i * (s // TSL) + j))
    return pl.pallas_call(
        _lru_kernel,
        grid=(b, s // TSL),
        in_specs=[
            tok, tok, const((1, d)), const((d, 3 * w)), const((CONV_WIDTH, w)), const((1, w)),
            const((LRU_HEADS, LRU_BLOCK, LRU_BLOCK)), const((1, w)),
            const((LRU_HEADS, LRU_BLOCK, LRU_BLOCK)), const((1, w)), const((1, w)),
            const((w, d)), const((d, d)), const((1, d)), const((N_EXPERTS, d)), const((N_EXPERTS, 1)),
            const((TSL, TSL)),
        ],
        out_specs=[tok, tok, rt, rt, const((N_EXPERTS, LANES))],
        out_shape=[
            jax.ShapeDtypeStruct((b, s, d), F32),
            jax.ShapeDtypeStruct((b, s, d), BF16),
            jax.ShapeDtypeStruct((SUBLANES, b * s), jnp.int32),
            jax.ShapeDtypeStruct((SUBLANES, b * s), F32),
            jax.ShapeDtypeStruct((N_EXPERTS, LANES), F32),
        ],
        scratch_shapes=[
            pltpu.VMEM((TSL + SUBLANES, w), F32),
            pltpu.VMEM(scan_shape, F32), pltpu.VMEM(scan_shape, F32),
            pltpu.VMEM(scan_shape, F32), pltpu.VMEM(scan_shape, F32),
            pltpu.VMEM((1, w), F32), pltpu.VMEM((N_EXPERTS, LANES), F32),
        ],
        compiler_params=pltpu.CompilerParams(
            dimension_semantics=("arbitrary", "arbitrary"), vmem_limit_bytes=VMEM_LIMIT),
        name="lru_router",
    )(x, ya, nw, wb, cw, cb, wra, bra, wrx, brx, lam, wpb, wout, nffn, wrt, brt, ustr)


def _invert_slots(slot_flat, n_rows, t):
    n = slot_flat.shape[0]
    vals = jnp.broadcast_to((jnp.arange(n, dtype=jnp.int32) % t)[:, None], (n, SC_ROW))
    mesh = plsc.VectorSubcoreMesh(core_axis_name="core", subcore_axis_name="subcore")

    @functools.partial(pl.kernel, out_type=jax.ShapeDtypeStruct((n_rows, SC_ROW), jnp.int32), mesh=mesh,
                       scratch_types=[])
    def scatter_kernel(x_hbm, i_hbm, o_hbm):
        def body(x_vmem, i_vmem):
            pltpu.sync_copy(x_vmem, o_hbm.at[i_vmem.at[0]])

        pltpu.emit_pipeline(
            body,
            grid=(n // SC_WINDOW,),
            in_specs=[pl.BlockSpec((SC_WINDOW, SC_ROW), lambda i: (i, 0)),
                      pl.BlockSpec((1, SC_WINDOW), lambda i: (0, i))],
            out_specs=[],
            core_axis_name=("core", "subcore"),
            dimension_semantics=(pltpu.PARALLEL,),
        )(x_hbm, i_hbm)

    return scatter_kernel(vals, slot_flat.reshape(1, n))[:, 0]


def _moe_kernel(te_ref, nu_ref, xs_ref, wgu_ref, bg_ref, bl_ref, wd_ref, bd_ref, perm_ref, *rest):
    y_ref, wg_s, wl_s, wd_s = rest[-4:]
    i = pl.program_id(0)
    half = MXU_COLS // 2
    new_expert = (i == 0) | (te_ref[i] != te_ref[jnp.maximum(i - 1, 0)])

    @pl.when(new_expert & (i < nu_ref[0]))
    def _():
        for blk in range(2 * D_FF // MXU_COLS):
            wblk = wgu_ref[0, :, blk * MXU_COLS:(blk + 1) * MXU_COLS].astype(BF16)
            pw = jnp.dot(wblk, perm_ref[...], preferred_element_type=F32)
            wg_s[:, blk * half:(blk + 1) * half] = pw[:, :half].astype(BF16)
            wl_s[:, blk * half:(blk + 1) * half] = pw[:, half:].astype(BF16)
        wd_s[...] = wd_ref[0].astype(BF16)

    @pl.when(i < nu_ref[0])
    def _():
        xs = xs_ref[...]
        g = jnp.dot(xs, wg_s[...], preferred_element_type=F32) + bg_ref[0]
        l = jnp.dot(xs, wl_s[...], preferred_element_type=F32) + bl_ref[0]
        g = jnp.minimum(g, SWIGLU_LIMIT)
        l = jnp.clip(l, -SWIGLU_LIMIT, SWIGLU_LIMIT)
        act = g * jax.nn.sigmoid(SWIGLU_ALPHA * g) * (l + 1.0)
        y = jnp.dot(act.astype(BF16), wd_s[...], preferred_element_type=F32) + bd_ref[0]
        y_ref[...] = y.astype(y_ref.dtype)

    @pl.when(i >= nu_ref[0])
    def _():
        y_ref[...] = jnp.zeros_like(y_ref)


def _moe_call(tile_expert, n_used, xs, wgu, bg, bl, wd, bd, *, y_prev=None, base_tile=0, total_rows=None):
    p, d = xs.shape
    n_tiles = p // TM
    total_rows = p if total_rows is None else total_rows
    tile_expert = lax.slice(tile_expert, (base_tile,), (base_tile + n_tiles,))
    n_used = jnp.clip(n_used - base_tile, 1, n_tiles)
    half = MXU_COLS // 2
    r = lax.broadcasted_iota(jnp.int32, (MXU_COLS, MXU_COLS), 0)
    c = lax.broadcasted_iota(jnp.int32, (MXU_COLS, MXU_COLS), 1)
    perm = (r == jnp.where(c < half, 2 * c, 2 * (c - half) + 1)).astype(BF16)
    row = lambda i, te, nu: (jnp.minimum(i, nu[0] - 1), 0)
    ex3 = lambda i, te, nu: (te[i], 0, 0)
    in_specs = [
        pl.BlockSpec((TM, d), row),
        pl.BlockSpec((1, d, 2 * D_FF), ex3),
        pl.BlockSpec((1, 1, D_FF), ex3), pl.BlockSpec((1, 1, D_FF), ex3),
        pl.BlockSpec((1, D_FF, d), ex3), pl.BlockSpec((1, 1, d), ex3),
        pl.BlockSpec((MXU_COLS, MXU_COLS), lambda i, te, nu: (0, 0)),
    ]
    operands = [tile_expert, n_used, xs, wgu, bg, bl, wd, bd, perm]
    aliases = {}
    if y_prev is not None:
        in_specs.append(pl.BlockSpec(memory_space=pl.ANY))
        aliases = {len(operands): 0}
        operands.append(y_prev)
    grid_spec = pltpu.PrefetchScalarGridSpec(
        num_scalar_prefetch=2,
        grid=(n_tiles,),
        in_specs=in_specs,
        out_specs=pl.BlockSpec((TM, d), lambda i, te, nu: (i + base_tile, 0)),
        scratch_shapes=[pltpu.VMEM((d, D_FF), BF16), pltpu.VMEM((d, D_FF), BF16), pltpu.VMEM((D_FF, d), BF16)],
    )
    return pl.pallas_call(
        _moe_kernel,
        grid_spec=grid_spec,
        out_shape=jax.ShapeDtypeStruct((total_rows, d), BF16),
        input_output_aliases=aliases,
        compiler_params=pltpu.CompilerParams(
            dimension_semantics=("arbitrary",), vmem_limit_bytes=VMEM_LIMIT),
        name="moe_experts",
    )(*operands)


def _final_kernel(x1_ref, yg_ref, gw_ref, nw_ref, *rest):
    out_ref = rest[-1]
    gw = gw_ref[...]
    x2 = x1_ref[...]
    for k in range(TOP_K):
        x2 = x2 + gw[:, k:k + 1] * yg_ref[k].astype(F32)
    out_ref[...] = _rms(x2, nw_ref[...])


def _final_call(x1, yg, gw, nw, *, out_prev=None, base_block=0):
    t, d = x1.shape
    blk = pl.BlockSpec((TF, d), lambda i: (i + base_block, 0))
    in_specs = [blk, pl.BlockSpec((TOP_K, TF, d), lambda i: (0, i, 0)),
                pl.BlockSpec((TF, TOP_K), lambda i: (i + base_block, 0)), pl.BlockSpec((1, d), lambda i: (0, 0))]
    operands = [x1, yg, gw, nw]
    aliases = {}
    if out_prev is not None:
        in_specs.append(pl.BlockSpec(memory_space=pl.ANY))
        aliases = {len(operands): 0}
        operands.append(out_prev)
    return pl.pallas_call(
        _final_kernel,
        grid=(yg.shape[1] // TF,),
        in_specs=in_specs,
        out_specs=blk,
        out_shape=jax.ShapeDtypeStruct((t, d), F32),
        input_output_aliases=aliases,
        compiler_params=pltpu.CompilerParams(dimension_semantics=("arbitrary",)),
        name="final_norm",
    )(*operands)


def _chunk_masks():
    r = lax.broadcasted_iota(jnp.int32, (TSH, TSH), 0)
    c = lax.broadcasted_iota(jnp.int32, (TSH, TSH), 1)
    tri = ((r // CHUNK == c // CHUNK) & (r >= c)).astype(BF16)
    r = lax.broadcasted_iota(jnp.int32, (TSL, TSL), 0)
    c = lax.broadcasted_iota(jnp.int32, (TSL, TSL), 1)
    ustr = (r < c).astype(BF16)
    return tri, ustr


def kernel(x, norm_mix, w_in, lb_raw, gn_w, conv_w, conv_b, w_rg_a, b_rg_a, w_rg_x, b_rg_x, lru_lambda,
           w_proj_a, w_proj_b, w_out, norm_ffn, w_router, b_router, w_gu, b_gu, w_down, b_down, norm_final):
    b, s, d = x.shape
    t = b * s
    assert w_in.shape[0] == 1 and s % TSH == 0 and s % TSL == 0 and t % (N_PARTS * TF) == 0
    assert (TOP_K * t // TM + N_EXPERTS) % N_PARTS == 0 and TOP_K * t // TM >= (N_PARTS - 1) * (
        TOP_K * t // TM + N_EXPERTS) // N_PARTS + 1
    hk = HA_HEADS * HA_DK
    w = LRU_WIDTH
    tri, ustr = _chunk_masks()

    wi = w_in[0]
    o_lru = 4 * hk
    wa = wi[:, 0:o_lru].astype(BF16)
    wb = jnp.concatenate([wi[:, o_lru:o_lru + 2 * w], wi[:, o_lru + 2 * w + d:o_lru + 2 * w + 2 * d]],
                         axis=1).astype(BF16)
    wga = wi[:, o_lru + 2 * w:o_lru + 2 * w + d].astype(BF16)
    row = lambda v: v.reshape(1, -1)

    ya = _hgrn_call(x, row(norm_mix[0]), wa, wga, lb_raw, row(gn_w[0]), w_proj_a[0].astype(BF16), tri)
    x1, h2, ri, rw, cnt = _lru_call(
        x, ya, row(norm_mix[0]), wb, conv_w[0], row(conv_b[0]), w_rg_a[0].astype(BF16), row(b_rg_a[0]),
        w_rg_x[0].astype(BF16), row(b_rg_x[0]), row(lru_lambda[0]), w_proj_b[0].astype(BF16),
        w_out[0].astype(BF16), row(norm_ffn[0]), w_router[0].T, b_router[0].reshape(-1, 1), ustr)

    idx, pos, gate_w = ri[0:TOP_K], ri[TOP_K:2 * TOP_K], rw[0:TOP_K]
    counts = cnt[:, 0].astype(jnp.int32)
    padded = ((counts + TM - 1) // TM) * TM
    ends = jnp.cumsum(padded)
    offs = ends - padded
    idx_d, slot = idx.reshape(-1, LANES), pos.reshape(-1, LANES)
    for e in range(N_EXPERTS):
        slot = slot + jnp.where(idx_d == e, offs[e], 0)
    slot = slot.reshape(TOP_K, t)
    n_rows = TOP_K * t + N_EXPERTS * TM
    n_tiles = n_rows // TM
    tile_expert = jnp.minimum(
        jnp.sum(jnp.arange(n_tiles, dtype=jnp.int32)[:, None] * TM >= ends[None, :], axis=1),
        N_EXPERTS - 1).astype(jnp.int32)
    n_used = (ends[-1:] // TM).astype(jnp.int32)

    n_pad = padded - counts
    pads_before = jnp.cumsum(n_pad) - n_pad
    j = jnp.arange(TM, dtype=jnp.int32)[None, :]
    e_base = (jnp.arange(N_EXPERTS, dtype=jnp.int32) * TM)[:, None]
    pad_slot = jnp.where(j < n_pad[:, None], (offs + counts)[:, None] + j,
                         ends[-1] + e_base + j - (pads_before + n_pad)[:, None])
    all_slots = jnp.concatenate([slot.reshape(-1), pad_slot.reshape(-1)])
    tok_of_slot = _invert_slots(all_slots, n_rows, t)
    h2r, x1r = h2.reshape(t, d), x1.reshape(t, d)
    part_tiles = n_tiles // N_PARTS
    part_rows = part_tiles * TM
    bgs, bls = b_gu[0][:, 0::2].reshape(N_EXPERTS, 1, D_FF), b_gu[0][:, 1::2].reshape(N_EXPERTS, 1, D_FF)
    bds = b_down[0].reshape(N_EXPERTS, 1, d)
    y = None
    for part in range(N_PARTS):
        xs = h2r.at[tok_of_slot[part * part_rows:(part + 1) * part_rows]].get(mode="promise_in_bounds")
        y = _moe_call(tile_expert, n_used, xs, w_gu[0], bgs, bls, w_down[0], bds,
                      y_prev=y, base_tile=part * part_tiles, total_rows=n_rows)

    gw = gate_w.T
    part_t = t // N_PARTS
    out = None
    for part in range(N_PARTS):
        yg = y.at[slot[:, part * part_t:(part + 1) * part_t]].get(mode="promise_in_bounds")
        out = _final_call(x1r, yg, gw, row(norm_final), out_prev=out, base_block=part * part_t // TF)
    return out.reshape(b, s, d)
```

```python
import functools

import jax
import jax.numpy as jnp
from jax import lax
from jax.experimental import pallas as pl
from jax.experimental.pallas import tpu as pltpu
from jax.experimental.pallas import tpu_sc as plsc

F32 = jnp.float32
BF16 = jnp.bfloat16

D_MODEL = 1024
HA_HEADS = 8
HA_DK = 128
HA_DV = 128
CHUNK = 64
LRU_WIDTH = 1024
LRU_HEADS = 4
LRU_BLOCK = LRU_WIDTH // LRU_HEADS
CONV_WIDTH = 4
RG_C = 8.0
N_EXPERTS = 32
TOP_K = 4
D_FF = 1024
SWIGLU_LIMIT = 7.0
SWIGLU_ALPHA = 1.702
EPS = 1e-5

SUBLANES = 8
LANES = 128
MXU_COLS = 256
TSH = 256
TSL = 512
TM = 512
TF = 512
N_GROUPS = 2
VMEM_LIMIT = 56 * 1024 * 1024
SC_WINDOW = 128
SC_ROW = 128

NT_DIMS = (((1,), (1,)), ((), ()))
TN_DIMS = (((0,), (0,)), ((), ()))


def _rms(x, w):
    return x * lax.rsqrt(jnp.mean(x * x, axis=-1, keepdims=True) + EPS) * w


def _split3(v):
    hi = v.astype(BF16)
    r1 = v - hi.astype(F32)
    mid = r1.astype(BF16)
    lo = (r1 - mid.astype(F32)).astype(BF16)
    return hi, mid, lo


def _hgrn_kernel(x_ref, nw_ref, wa_ref, wga_ref, lbraw_ref, gnw_ref, wpa_ref, tri_ref, after_ref, out_ref,
                 st_ref, qin_ref, kin_ref, kdec_ref, v_ref, g_ref, oa_ref):
    del after_ref
    n_chunks = TSH // CHUNK
    hk = HA_HEADS * HA_DK

    @pl.when(pl.program_id(1) == 0)
    def _():
        st_ref[...] = jnp.zeros_like(st_ref)

    hb = _rms(x_ref[0], nw_ref[...]).astype(BF16)

    lbr = lbraw_ref[...]
    rows = [lbr[j:j + 1, :] for j in range(lbr.shape[0])]
    mx = functools.reduce(jnp.maximum, rows)
    es = [jnp.exp(r - mx) for r in rows]
    lb = es[0] / functools.reduce(lambda a, b: a + b, es)

    fl = jnp.dot(hb, wa_ref[:, hk:2 * hk], preferred_element_type=F32)
    f = lb + (1.0 - lb) * jax.nn.sigmoid(fl)
    lf = jnp.log(f)
    k = 1.0 - f
    tri = tri_ref[...]
    bcum = functools.reduce(lambda a, b: a + b,
                            [jnp.dot(tri, p, preferred_element_type=F32) for p in _split3(lf)])
    b_last = [bcum[c * CHUNK + CHUNK - 1:c * CHUNK + CHUNK, :] for c in range(n_chunks)]
    bl_full = jnp.concatenate([jnp.broadcast_to(b, (CHUNK, hk)) for b in b_last], axis=0)

    q = jnp.dot(hb, wa_ref[:, 0:hk], preferred_element_type=F32)
    qin_ref[...] = (jax.nn.silu(q) * jnp.exp(bcum)).astype(BF16)
    kin_ref[...] = (k * jnp.exp(-bcum)).astype(BF16)
    kdec_ref[...] = (k * jnp.exp(bl_full - bcum)).astype(BF16)
    v_ref[...] = jnp.dot(hb, wa_ref[:, 2 * hk:3 * hk], preferred_element_type=F32).astype(BF16)
    g = jnp.dot(hb, wa_ref[:, 3 * hk:4 * hk], preferred_element_type=F32)
    g_ref[...] = jax.nn.silu(g) * gnw_ref[...]

    causal = (lax.broadcasted_iota(jnp.int32, (CHUNK, CHUNK), 0)
              >= lax.broadcasted_iota(jnp.int32, (CHUNK, CHUNK), 1))
    sts = [st_ref[h] for h in range(HA_HEADS)]
    for c in range(n_chunks):
        r = slice(c * CHUNK, (c + 1) * CHUNK)
        edec = jnp.exp(b_last[c])
        for h in range(HA_HEADS):
            l = slice(h * HA_DK, (h + 1) * HA_DK)
            qh, kh, kd, vh = qin_ref[r, l], kin_ref[r, l], kdec_ref[r, l], v_ref[r, l]
            s = lax.dot_general(qh, kh, NT_DIMS, preferred_element_type=F32)
            s = jnp.where(causal, s, 0.0).astype(BF16)
            o = (jnp.dot(s, vh, preferred_element_type=F32)
                 + jnp.dot(qh, sts[h].T.astype(BF16), preferred_element_type=F32))
            sts[h] = sts[h] * edec[:, l] + lax.dot_general(vh, kd, TN_DIMS, preferred_element_type=F32)
            ms = jnp.mean(o * o, axis=-1, keepdims=True)
            oa_ref[r, l] = (o * lax.rsqrt(ms + EPS) * g_ref[r, l]).astype(BF16)
    for h in range(HA_HEADS):
        st_ref[h] = sts[h]

    ya = jnp.dot(oa_ref[...], wpa_ref[...], preferred_element_type=F32)
    ga = jnp.dot(hb, wga_ref[...], preferred_element_type=F32)
    out_ref[0] = jax.nn.sigmoid(ga) * ya


def _hgrn_call(x, nw, wa, wga, lb_raw, gnw, wpa, tri, after, *, batch_base, n_batch):
    _, s, d = x.shape
    b = n_batch
    hk = HA_HEADS * HA_DK
    const = lambda shape: pl.BlockSpec(shape, lambda i, j: (0,) * len(shape))
    return pl.pallas_call(
        _hgrn_kernel,
        grid=(b, s // TSH),
        in_specs=[
            pl.BlockSpec((1, TSH, d), lambda i, j: (i + batch_base, j, 0)),
            const((1, d)), const((d, 4 * hk)), const((d, d)), const(lb_raw.shape), const((1, hk)),
            const((hk, d)), const((TSH, TSH)), pl.BlockSpec(memory_space=pl.ANY),
        ],
        out_specs=pl.BlockSpec((1, TSH, d), lambda i, j: (i, j, 0)),
        out_shape=jax.ShapeDtypeStruct((b, s, d), F32),
        scratch_shapes=[
            pltpu.VMEM((HA_HEADS, HA_DV, HA_DK), F32),
            pltpu.VMEM((TSH, hk), BF16), pltpu.VMEM((TSH, hk), BF16), pltpu.VMEM((TSH, hk), BF16),
            pltpu.VMEM((TSH, hk), BF16), pltpu.VMEM((TSH, hk), F32), pltpu.VMEM((TSH, hk), BF16),
        ],
        compiler_params=pltpu.CompilerParams(
            dimension_semantics=("arbitrary", "arbitrary"), vmem_limit_bytes=VMEM_LIMIT),
        name="hgrn",
    )(x, nw, wa, wga, lb_raw, gnw, wpa, tri, after)


def _lru_kernel(x_ref, ya_ref, nw_ref, wb_ref, cw_ref, cb_ref, wra_ref, bra_ref, wrx_ref, brx_ref, lam_ref,
                wpb_ref, wout_ref, nffn_ref, wrt_ref, brt_ref, ustr_ref,
                x1_ref, h2_ref, ri_ref, rw_ref, cnt_ref,
                xbuf_ref, a_ref, u_ref, hl_ref, pp_ref, hc_ref, carry_ref):
    seg = TSL // SUBLANES
    pitch = seg + SUBLANES
    w = LRU_WIDTH
    bi, si = pl.program_id(0), pl.program_id(1)

    @pl.when(si == 0)
    def _():
        xbuf_ref[0:SUBLANES, :] = jnp.zeros((SUBLANES, w), F32)
        hc_ref[...] = jnp.zeros_like(hc_ref)

    @pl.when((bi == 0) & (si == 0))
    def _():
        carry_ref[...] = jnp.zeros_like(carry_ref)

    x = x_ref[0]
    hb = _rms(x, nw_ref[...]).astype(BF16)

    xb = jnp.dot(hb, wb_ref[:, 0:w], preferred_element_type=F32)
    xbuf_ref[SUBLANES:SUBLANES + TSL, :] = xb
    cw = cw_ref[...]
    xc = cb_ref[...] + cw[CONV_WIDTH - 1:CONV_WIDTH, :] * xb
    for kk in range(CONV_WIDTH - 1):
        shift = CONV_WIDTH - 1 - kk
        xc = xc + cw[kk:kk + 1, :] * xbuf_ref[SUBLANES - shift:SUBLANES - shift + TSL, :]
    xbuf_ref[0:SUBLANES, :] = xbuf_ref[TSL:TSL + SUBLANES, :]

    xcb = xc.astype(BF16)
    r_parts, i_parts = [], []
    for j in range(LRU_HEADS):
        l = slice(j * LRU_BLOCK, (j + 1) * LRU_BLOCK)
        r_parts.append(jnp.dot(xcb[:, l], wra_ref[j], preferred_element_type=F32))
        i_parts.append(jnp.dot(xcb[:, l], wrx_ref[j], preferred_element_type=F32))
    rg = jax.nn.sigmoid(jnp.concatenate(r_parts, axis=1) + bra_ref[...])
    ig = jax.nn.sigmoid(jnp.concatenate(i_parts, axis=1) + brx_ref[...])
    log_a = RG_C * rg * jax.nn.log_sigmoid(lam_ref[...])
    a = jnp.exp(log_a)
    m2 = 1.0 - a * a
    mult = jnp.where(m2 > 0.0, m2 * lax.rsqrt(m2), 0.0)
    is_first = (lax.broadcasted_iota(jnp.int32, (TSL, 1), 0) == 0) & (si == 0)
    mult = jnp.where(is_first, 1.0, mult)
    u = xc * ig * mult
    n_groups = w // LANES
    group_rows = SUBLANES * pitch
    for gi in range(n_groups):
        for j in range(SUBLANES):
            dst = slice(gi * group_rows + j * pitch, gi * group_rows + j * pitch + seg)
            a_ref[dst, :] = a[j * seg:(j + 1) * seg, gi * LANES:(gi + 1) * LANES]
            u_ref[dst, :] = u[j * seg:(j + 1) * seg, gi * LANES:(gi + 1) * LANES]

    def step(i, hp):
        hs, ps = hp
        new_h, new_p = [], []
        for gi in range(n_groups):
            rows = pl.ds(gi * group_rows + i, SUBLANES, stride=pitch)
            ai = a_ref[rows, :]
            h = ai * hs[gi] + u_ref[rows, :]
            p = ai * ps[gi]
            hl_ref[rows, :] = h
            pp_ref[rows, :] = p
            new_h.append(h)
            new_p.append(p)
        return tuple(new_h), tuple(new_p)

    init = (tuple(jnp.zeros((SUBLANES, LANES), F32) for _ in range(n_groups)),
            tuple(jnp.ones((SUBLANES, LANES), F32) for _ in range(n_groups)))
    h_end, p_end = lax.fori_loop(0, seg, step, init, unroll=4)
    h_end = jnp.concatenate(h_end, axis=1)
    p_end = jnp.concatenate(p_end, axis=1)
    c = hc_ref[...]
    h_parts = []
    for j in range(SUBLANES):
        src = [slice(gi * group_rows + j * pitch, gi * group_rows + j * pitch + seg) for gi in range(n_groups)]
        hl = jnp.concatenate([hl_ref[r, :] for r in src], axis=1)
        pp = jnp.concatenate([pp_ref[r, :] for r in src], axis=1)
        h_parts.append(hl + pp * c)
        c = h_end[j:j + 1, :] + p_end[j:j + 1, :] * c
    hc_ref[...] = c
    h_full = jnp.concatenate(h_parts, axis=0)

    gate_b = jnp.dot(hb, wb_ref[:, w:2 * w], preferred_element_type=F32)
    ob = (h_full * jax.nn.gelu(gate_b)).astype(BF16)
    yb = jnp.dot(ob, wpb_ref[...], preferred_element_type=F32)
    gb = jnp.dot(hb, wb_ref[:, 2 * w:3 * w], preferred_element_type=F32)
    merged = ya_ref[0] + jax.nn.sigmoid(gb) * yb
    x1 = x + jnp.dot(merged.astype(BF16), wout_ref[...], preferred_element_type=F32)
    x1_ref[0] = x1

    h2 = _rms(x1, nffn_ref[...])
    h_hi = h2.astype(BF16)
    h2_ref[0] = h_hi.astype(F32)
    h_lo = (h2 - h_hi.astype(F32)).astype(BF16)
    wt = wrt_ref[...]
    w_hi = wt.astype(BF16)
    w_lo = (wt - w_hi.astype(F32)).astype(BF16)
    both = lax.dot_general(jnp.concatenate([w_hi, w_lo], axis=0), h_hi, NT_DIMS, preferred_element_type=F32)
    logits = (both[:N_EXPERTS] + both[N_EXPERTS:]
              + lax.dot_general(w_hi, h_lo, NT_DIMS, preferred_element_type=F32)) + brt_ref[...]

    eio = lax.broadcasted_iota(jnp.int32, (N_EXPERTS, TSL), 0)
    idxs, vals = [], []
    for _ in range(TOP_K):
        m = jnp.max(logits, axis=0, keepdims=True)
        ik = jnp.min(jnp.where(logits == m, eio, N_EXPERTS), axis=0, keepdims=True)
        idxs.append(ik)
        vals.append(m)
        logits = jnp.where(eio == ik, -jnp.inf, logits)
    exps = [jnp.exp(v - vals[0]) for v in vals]
    den = functools.reduce(lambda p, q: p + q, exps)
    onehots = [(eio == ik) for ik in idxs]
    oh = functools.reduce(lambda p, q: p + q, [o.astype(F32) for o in onehots])
    cum = jnp.dot(oh.astype(BF16), ustr_ref[...], preferred_element_type=F32) + carry_ref[:, 0:1]
    poss = [jnp.sum(jnp.where(o, cum, 0.0), axis=0, keepdims=True).astype(jnp.int32) for o in onehots]
    carry_ref[...] = carry_ref[...] + jnp.sum(oh, axis=1, keepdims=True)
    ri_ref[...] = jnp.concatenate(idxs + poss, axis=0)
    rw_ref[...] = jnp.concatenate([e / den for e in exps] + [jnp.zeros((SUBLANES - TOP_K, TSL), F32)], axis=0)
    cnt_ref[...] = carry_ref[...]


def _lru_call(x, ya, nw, wb, cw, cb, wra, bra, wrx, brx, lam, wpb, wout, nffn, wrt, brt, ustr, *, batch_base):
    b, s, d = ya.shape
    w = LRU_WIDTH
    scan_shape = ((TSL + SUBLANES * SUBLANES) * (w // LANES), LANES)
    const = lambda shape: pl.BlockSpec(shape, lambda i, j: (0,) * len(shape))
    tok = pl.BlockSpec((1, TSL, d), lambda i, j: (i, j, 0))
    tok_x = pl.BlockSpec((1, TSL, d), lambda i, j: (i + batch_base, j, 0))
    rt = pl.BlockSpec((SUBLANES, TSL), lambda i, j: (0, i * (s // TSL) + j))
    return pl.pallas_call(
        _lru_kernel,
        grid=(b, s // TSL),
        in_specs=[
            tok_x, tok, const((1, d)), const((d, 3 * w)), const((CONV_WIDTH, w)), const((1, w)),
            const((LRU_HEADS, LRU_BLOCK, LRU_BLOCK)), const((1, w)),
            const((LRU_HEADS, LRU_BLOCK, LRU_BLOCK)), const((1, w)), const((1, w)),
            const((w, d)), const((d, d)), const((1, d)), const((N_EXPERTS, d)), const((N_EXPERTS, 1)),
            const((TSL, TSL)),
        ],
        out_specs=[tok, tok, rt, rt, const((N_EXPERTS, LANES))],
        out_shape=[
            jax.ShapeDtypeStruct((b, s, d), F32),
            jax.ShapeDtypeStruct((b, s, d), F32),
            jax.ShapeDtypeStruct((SUBLANES, b * s), jnp.int32),
            jax.ShapeDtypeStruct((SUBLANES, b * s), F32),
            jax.ShapeDtypeStruct((N_EXPERTS, LANES), F32),
        ],
        scratch_shapes=[
            pltpu.VMEM((TSL + SUBLANES, w), F32),
            pltpu.VMEM(scan_shape, F32), pltpu.VMEM(scan_shape, F32),
            pltpu.VMEM(scan_shape, F32), pltpu.VMEM(scan_shape, F32),
            pltpu.VMEM((1, w), F32), pltpu.VMEM((N_EXPERTS, LANES), F32),
        ],
        compiler_params=pltpu.CompilerParams(
            dimension_semantics=("arbitrary", "arbitrary"), vmem_limit_bytes=VMEM_LIMIT),
        name="lru_router",
    )(x, ya, nw, wb, cw, cb, wra, bra, wrx, brx, lam, wpb, wout, nffn, wrt, brt, ustr)


def _invert_slots(slot_flat, n_rows, t):
    n = slot_flat.shape[0]
    vals = jnp.broadcast_to((jnp.arange(n, dtype=jnp.int32) % t)[:, None], (n, SC_ROW))
    mesh = plsc.VectorSubcoreMesh(core_axis_name="core", subcore_axis_name="subcore")

    @functools.partial(pl.kernel, out_type=jax.ShapeDtypeStruct((n_rows, SC_ROW), jnp.int32), mesh=mesh,
                       scratch_types=[])
    def scatter_kernel(x_hbm, i_hbm, o_hbm):
        def body(x_vmem, i_vmem):
            pltpu.sync_copy(x_vmem, o_hbm.at[i_vmem.at[0]])

        pltpu.emit_pipeline(
            body,
            grid=(n // SC_WINDOW,),
            in_specs=[pl.BlockSpec((SC_WINDOW, SC_ROW), lambda i: (i, 0)),
                      pl.BlockSpec((1, SC_WINDOW), lambda i: (0, i))],
            out_specs=[],
            core_axis_name=("core", "subcore"),
            dimension_semantics=(pltpu.PARALLEL,),
        )(x_hbm, i_hbm)

    return scatter_kernel(vals, slot_flat.reshape(1, n))[:, 0]


def _moe_kernel(te_ref, nu_ref, xs_ref, wgu_ref, bg_ref, bl_ref, wd_ref, bd_ref, perm_ref, y_ref,
                wg_s, wl_s, wd_s):
    i = pl.program_id(0)
    half = MXU_COLS // 2
    new_expert = (i == 0) | (te_ref[i] != te_ref[jnp.maximum(i - 1, 0)])

    @pl.when(new_expert & (i < nu_ref[0]))
    def _():
        for blk in range(2 * D_FF // MXU_COLS):
            wblk = wgu_ref[0, :, blk * MXU_COLS:(blk + 1) * MXU_COLS].astype(BF16)
            pw = jnp.dot(wblk, perm_ref[...], preferred_element_type=F32)
            wg_s[:, blk * half:(blk + 1) * half] = pw[:, :half].astype(BF16)
            wl_s[:, blk * half:(blk + 1) * half] = pw[:, half:].astype(BF16)
        wd_s[...] = wd_ref[0].astype(BF16)

    @pl.when(i < nu_ref[0])
    def _():
        xs = xs_ref[...].astype(BF16)
        g = jnp.dot(xs, wg_s[...], preferred_element_type=F32) + bg_ref[0]
        l = jnp.dot(xs, wl_s[...], preferred_element_type=F32) + bl_ref[0]
        g = jnp.minimum(g, SWIGLU_LIMIT)
        l = jnp.clip(l, -SWIGLU_LIMIT, SWIGLU_LIMIT)
        act = g * jax.nn.sigmoid(SWIGLU_ALPHA * g) * (l + 1.0)
        y = jnp.dot(act.astype(BF16), wd_s[...], preferred_element_type=F32) + bd_ref[0]
        y_ref[...] = y.astype(y_ref.dtype)

    @pl.when(i >= nu_ref[0])
    def _():
        y_ref[...] = jnp.zeros_like(y_ref)


def _moe_call(tile_expert, n_used, xs, wgu, bg, bl, wd, bd):
    p, d = xs.shape
    n_tiles = p // TM
    half = MXU_COLS // 2
    r = lax.broadcasted_iota(jnp.int32, (MXU_COLS, MXU_COLS), 0)
    c = lax.broadcasted_iota(jnp.int32, (MXU_COLS, MXU_COLS), 1)
    perm = (r == jnp.where(c < half, 2 * c, 2 * (c - half) + 1)).astype(BF16)
    row = lambda i, te, nu: (jnp.minimum(i, nu[0] - 1), 0)
    ex3 = lambda i, te, nu: (te[i], 0, 0)
    grid_spec = pltpu.PrefetchScalarGridSpec(
        num_scalar_prefetch=2,
        grid=(n_tiles,),
        in_specs=[
            pl.BlockSpec((TM, d), row),
            pl.BlockSpec((1, d, 2 * D_FF), ex3),
            pl.BlockSpec((1, 1, D_FF), ex3), pl.BlockSpec((1, 1, D_FF), ex3),
            pl.BlockSpec((1, D_FF, d), ex3), pl.BlockSpec((1, 1, d), ex3),
            pl.BlockSpec((MXU_COLS, MXU_COLS), lambda i, te, nu: (0, 0)),
        ],
        out_specs=pl.BlockSpec((TM, d), lambda i, te, nu: (i, 0)),
        scratch_shapes=[pltpu.VMEM((d, D_FF), BF16), pltpu.VMEM((d, D_FF), BF16), pltpu.VMEM((D_FF, d), BF16)],
    )
    return pl.pallas_call(
        _moe_kernel,
        grid_spec=grid_spec,
        out_shape=jax.ShapeDtypeStruct((p, d), BF16),
        compiler_params=pltpu.CompilerParams(
            dimension_semantics=("arbitrary",), vmem_limit_bytes=VMEM_LIMIT),
        name="moe_experts",
    )(tile_expert, n_used, xs, wgu, bg, bl, wd, bd, perm)


def _final_kernel(x1_ref, yg_ref, gw_ref, nw_ref, *rest):
    out_ref = rest[-1]
    gw = gw_ref[...]
    x2 = x1_ref[...]
    for k in range(TOP_K):
        x2 = x2 + gw[:, k:k + 1] * yg_ref[k].astype(F32)
    out_ref[...] = _rms(x2, nw_ref[...])


def _final_call(x1, yg, gw, nw, *, out_prev=None, base_block=0, total_rows=None):
    t, d = x1.shape
    blk = pl.BlockSpec((TF, d), lambda i: (i, 0))
    in_specs = [blk, pl.BlockSpec((TOP_K, TF, d), lambda i: (0, i, 0)),
                pl.BlockSpec((TF, TOP_K), lambda i: (i, 0)), pl.BlockSpec((1, d), lambda i: (0, 0))]
    operands = [x1, yg, gw, nw]
    aliases = {}
    if out_prev is not None:
        in_specs.append(pl.BlockSpec(memory_space=pl.ANY))
        aliases = {len(operands): 0}
        operands.append(out_prev)
    return pl.pallas_call(
        _final_kernel,
        grid=(t // TF,),
        in_specs=in_specs,
        out_specs=pl.BlockSpec((TF, d), lambda i: (i + base_block, 0)),
        out_shape=jax.ShapeDtypeStruct((t if total_rows is None else total_rows, d), F32),
        input_output_aliases=aliases,
        compiler_params=pltpu.CompilerParams(dimension_semantics=("arbitrary",)),
        name="final_norm",
    )(*operands)


def _chunk_masks():
    r = lax.broadcasted_iota(jnp.int32, (TSH, TSH), 0)
    c = lax.broadcasted_iota(jnp.int32, (TSH, TSH), 1)
    tri = ((r // CHUNK == c // CHUNK) & (r >= c)).astype(BF16)
    r = lax.broadcasted_iota(jnp.int32, (TSL, TSL), 0)
    c = lax.broadcasted_iota(jnp.int32, (TSL, TSL), 1)
    ustr = (r < c).astype(BF16)
    return tri, ustr


def kernel(x, norm_mix, w_in, lb_raw, gn_w, conv_w, conv_b, w_rg_a, b_rg_a, w_rg_x, b_rg_x, lru_lambda,
           w_proj_a, w_proj_b, w_out, norm_ffn, w_router, b_router, w_gu, b_gu, w_down, b_down, norm_final):
    b, s, d = x.shape
    assert w_in.shape[0] == 1 and s % TSH == 0 and s % TSL == 0 and b % N_GROUPS == 0
    gb = b // N_GROUPS
    t = gb * s
    assert t % TF == 0 and (TOP_K * t) % TM == 0
    hk = HA_HEADS * HA_DK
    w = LRU_WIDTH
    tri, ustr = _chunk_masks()

    wi = w_in[0]
    o_lru = 4 * hk
    wa = wi[:, 0:o_lru].astype(BF16)
    wb = jnp.concatenate([wi[:, o_lru:o_lru + 2 * w], wi[:, o_lru + 2 * w + d:o_lru + 2 * w + 2 * d]],
                         axis=1).astype(BF16)
    wga = wi[:, o_lru + 2 * w:o_lru + 2 * w + d].astype(BF16)
    row = lambda v: v.reshape(1, -1)

    wpa, wpb, wout = w_proj_a[0].astype(BF16), w_proj_b[0].astype(BF16), w_out[0].astype(BF16)
    wra, wrx = w_rg_a[0].astype(BF16), w_rg_x[0].astype(BF16)
    bgs, bls = b_gu[0][:, 0::2].reshape(N_EXPERTS, 1, D_FF), b_gu[0][:, 1::2].reshape(N_EXPERTS, 1, D_FF)
    bds = b_down[0].reshape(N_EXPERTS, 1, d)
    n_rows = TOP_K * t + N_EXPERTS * TM
    n_tiles = n_rows // TM

    def mixer_and_dispatch(g, after):
        ya = _hgrn_call(x, row(norm_mix[0]), wa, wga, lb_raw, row(gn_w[0]), wpa, tri, after,
                        batch_base=g * gb, n_batch=gb)
        x1, h2, ri, rw, cnt = _lru_call(
            x, ya, row(norm_mix[0]), wb, conv_w[0], row(conv_b[0]), wra, row(b_rg_a[0]), wrx, row(b_rg_x[0]),
            row(lru_lambda[0]), wpb, wout, row(norm_ffn[0]), w_router[0].T, b_router[0].reshape(-1, 1), ustr,
            batch_base=g * gb)

        idx, pos, gate_w = ri[0:TOP_K], ri[TOP_K:2 * TOP_K], rw[0:TOP_K]
        counts = cnt[:, 0].astype(jnp.int32)
        padded = ((counts + TM - 1) // TM) * TM
        ends = jnp.cumsum(padded)
        offs = ends - padded
        idx_d, slot = idx.reshape(-1, LANES), pos.reshape(-1, LANES)
        for e in range(N_EXPERTS):
            slot = slot + jnp.where(idx_d == e, offs[e], 0)
        slot = slot.reshape(TOP_K, t)
        tile_expert = jnp.minimum(
            jnp.sum(jnp.arange(n_tiles, dtype=jnp.int32)[:, None] * TM >= ends[None, :], axis=1),
            N_EXPERTS - 1).astype(jnp.int32)
        n_used = (ends[-1:] // TM).astype(jnp.int32)

        n_pad = padded - counts
        pads_before = jnp.cumsum(n_pad) - n_pad
        j = jnp.arange(TM, dtype=jnp.int32)[None, :]
        e_base = (jnp.arange(N_EXPERTS, dtype=jnp.int32) * TM)[:, None]
        pad_slot = jnp.where(j < n_pad[:, None], (offs + counts)[:, None] + j,
                             ends[-1] + e_base + j - (pads_before + n_pad)[:, None])
        all_slots = jnp.concatenate([slot.reshape(-1), pad_slot.reshape(-1)])
        tok_of_slot = _invert_slots(all_slots, n_rows, t)
        xs = h2.reshape(t, d).at[tok_of_slot].get(mode="promise_in_bounds")
        return (x1.reshape(t, d), xs, tile_expert, n_used, slot, gate_w.T), tok_of_slot

    groups, after = [], jnp.zeros((n_rows,), jnp.int32)
    for g in range(N_GROUPS):
        group, after = mixer_and_dispatch(g, after)
        groups.append(group)
    ys = [_moe_call(te, nu, xs, w_gu[0], bgs, bls, w_down[0], bds) for (_, xs, te, nu, _, _) in groups]
    out = None
    for g, ((x1r, _, _, _, slot, gw), y) in enumerate(zip(groups, ys)):
        yg = y.at[slot].get(mode="promise_in_bounds")
        out = _final_call(x1r, yg, gw, row(norm_final), out_prev=out, base_block=g * t // TF,
                          total_rows=N_GROUPS * t)
    return out.reshape(b, s, d)
```

```python
import functools

import jax
import jax.numpy as jnp
from jax import lax
from jax.experimental import pallas as pl
from jax.experimental.pallas import tpu as pltpu
from jax.experimental.pallas import tpu_sc as plsc

F32 = jnp.float32
BF16 = jnp.bfloat16

D_MODEL = 1024
HA_HEADS = 8
HA_DK = 128
HA_DV = 128
CHUNK = 64
LRU_WIDTH = 1024
LRU_HEADS = 4
LRU_BLOCK = LRU_WIDTH // LRU_HEADS
CONV_WIDTH = 4
RG_C = 8.0
N_EXPERTS = 32
TOP_K = 4
D_FF = 1024
SWIGLU_LIMIT = 7.0
SWIGLU_ALPHA = 1.702
EPS = 1e-5

SUBLANES = 8
LANES = 128
MXU_COLS = 256
TSH = 256
TSL = 512
TM = 512
TF = 512
MOE_PARTS = (1, 3)
FINAL_PARTS = (1, 1, 2)
VMEM_LIMIT = 56 * 1024 * 1024
SC_WINDOW = 128
SC_ROW = 128

NT_DIMS = (((1,), (1,)), ((), ()))
TN_DIMS = (((0,), (0,)), ((), ()))


def _rms(x, w):
    return x * lax.rsqrt(jnp.mean(x * x, axis=-1, keepdims=True) + EPS) * w


def _split3(v):
    hi = v.astype(BF16)
    r1 = v - hi.astype(F32)
    mid = r1.astype(BF16)
    lo = (r1 - mid.astype(F32)).astype(BF16)
    return hi, mid, lo


def _hgrn_kernel(x_ref, nw_ref, wa_ref, wga_ref, lbraw_ref, gnw_ref, wpa_ref, tri_ref, out_ref,
                 st_ref, qin_ref, kin_ref, kdec_ref, v_ref, g_ref, oa_ref):
    n_chunks = TSH // CHUNK
    hk = HA_HEADS * HA_DK

    @pl.when(pl.program_id(1) == 0)
    def _():
        st_ref[...] = jnp.zeros_like(st_ref)

    hb = _rms(x_ref[0], nw_ref[...]).astype(BF16)

    lbr = lbraw_ref[...]
    rows = [lbr[j:j + 1, :] for j in range(lbr.shape[0])]
    mx = functools.reduce(jnp.maximum, rows)
    es = [jnp.exp(r - mx) for r in rows]
    lb = es[0] / functools.reduce(lambda a, b: a + b, es)

    fl = jnp.dot(hb, wa_ref[:, hk:2 * hk], preferred_element_type=F32)
    f = lb + (1.0 - lb) * jax.nn.sigmoid(fl)
    lf = jnp.log(f)
    k = 1.0 - f
    tri = tri_ref[...]
    bcum = functools.reduce(lambda a, b: a + b,
                            [jnp.dot(tri, p, preferred_element_type=F32) for p in _split3(lf)])
    b_last = [bcum[c * CHUNK + CHUNK - 1:c * CHUNK + CHUNK, :] for c in range(n_chunks)]
    bl_full = jnp.concatenate([jnp.broadcast_to(b, (CHUNK, hk)) for b in b_last], axis=0)

    q = jnp.dot(hb, wa_ref[:, 0:hk], preferred_element_type=F32)
    qin_ref[...] = (jax.nn.silu(q) * jnp.exp(bcum)).astype(BF16)
    kin_ref[...] = (k * jnp.exp(-bcum)).astype(BF16)
    kdec_ref[...] = (k * jnp.exp(bl_full - bcum)).astype(BF16)
    v_ref[...] = jnp.dot(hb, wa_ref[:, 2 * hk:3 * hk], preferred_element_type=F32).astype(BF16)
    g = jnp.dot(hb, wa_ref[:, 3 * hk:4 * hk], preferred_element_type=F32)
    g_ref[...] = jax.nn.silu(g) * gnw_ref[...]

    causal = (lax.broadcasted_iota(jnp.int32, (CHUNK, CHUNK), 0)
              >= lax.broadcasted_iota(jnp.int32, (CHUNK, CHUNK), 1))
    sts = [st_ref[h] for h in range(HA_HEADS)]
    for c in range(n_chunks):
        r = slice(c * CHUNK, (c + 1) * CHUNK)
        edec = jnp.exp(b_last[c])
        for h in range(HA_HEADS):
            l = slice(h * HA_DK, (h + 1) * HA_DK)
            qh, kh, kd, vh = qin_ref[r, l], kin_ref[r, l], kdec_ref[r, l], v_ref[r, l]
            s = lax.dot_general(qh, kh, NT_DIMS, preferred_element_type=F32)
            s = jnp.where(causal, s, 0.0).astype(BF16)
            o = (jnp.dot(s, vh, preferred_element_type=F32)
                 + jnp.dot(qh, sts[h].T.astype(BF16), preferred_element_type=F32))
            sts[h] = sts[h] * edec[:, l] + lax.dot_general(vh, kd, TN_DIMS, preferred_element_type=F32)
            ms = jnp.mean(o * o, axis=-1, keepdims=True)
            oa_ref[r, l] = (o * lax.rsqrt(ms + EPS) * g_ref[r, l]).astype(BF16)
    for h in range(HA_HEADS):
        st_ref[h] = sts[h]

    ya = jnp.dot(oa_ref[...], wpa_ref[...], preferred_element_type=F32)
    ga = jnp.dot(hb, wga_ref[...], preferred_element_type=F32)
    out_ref[0] = jax.nn.sigmoid(ga) * ya


def _hgrn_call(x, nw, wa, wga, lb_raw, gnw, wpa, tri):
    b, s, d = x.shape
    hk = HA_HEADS * HA_DK
    const = lambda shape: pl.BlockSpec(shape, lambda i, j: (0,) * len(shape))
    return pl.pallas_call(
        _hgrn_kernel,
        grid=(b, s // TSH),
        in_specs=[
            pl.BlockSpec((1, TSH, d), lambda i, j: (i, j, 0)),
            const((1, d)), const((d, 4 * hk)), const((d, d)), const(lb_raw.shape), const((1, hk)),
            const((hk, d)), const((TSH, TSH)),
        ],
        out_specs=pl.BlockSpec((1, TSH, d), lambda i, j: (i, j, 0)),
        out_shape=jax.ShapeDtypeStruct((b, s, d), F32),
        scratch_shapes=[
            pltpu.VMEM((HA_HEADS, HA_DV, HA_DK), F32),
            pltpu.VMEM((TSH, hk), BF16), pltpu.VMEM((TSH, hk), BF16), pltpu.VMEM((TSH, hk), BF16),
            pltpu.VMEM((TSH, hk), BF16), pltpu.VMEM((TSH, hk), F32), pltpu.VMEM((TSH, hk), BF16),
        ],
        compiler_params=pltpu.CompilerParams(
            dimension_semantics=("arbitrary", "arbitrary"), vmem_limit_bytes=VMEM_LIMIT),
        name="hgrn",
    )(x, nw, wa, wga, lb_raw, gnw, wpa, tri)


def _lru_kernel(x_ref, ya_ref, nw_ref, wb_ref, cw_ref, cb_ref, wra_ref, bra_ref, wrx_ref, brx_ref, lam_ref,
                wpb_ref, wout_ref, nffn_ref, wrt_ref, brt_ref, ustr_ref,
                x1_ref, h2_ref, ri_ref, rw_ref, cnt_ref,
                xbuf_ref, a_ref, u_ref, hl_ref, pp_ref, hc_ref, carry_ref):
    seg = TSL // SUBLANES
    pitch = seg + SUBLANES
    w = LRU_WIDTH
    bi, si = pl.program_id(0), pl.program_id(1)

    @pl.when(si == 0)
    def _():
        xbuf_ref[0:SUBLANES, :] = jnp.zeros((SUBLANES, w), F32)
        hc_ref[...] = jnp.zeros_like(hc_ref)

    @pl.when((bi == 0) & (si == 0))
    def _():
        carry_ref[...] = jnp.zeros_like(carry_ref)

    x = x_ref[0]
    hb = _rms(x, nw_ref[...]).astype(BF16)

    xb = jnp.dot(hb, wb_ref[:, 0:w], preferred_element_type=F32)
    xbuf_ref[SUBLANES:SUBLANES + TSL, :] = xb
    cw = cw_ref[...]
    xc = cb_ref[...] + cw[CONV_WIDTH - 1:CONV_WIDTH, :] * xb
    for kk in range(CONV_WIDTH - 1):
        shift = CONV_WIDTH - 1 - kk
        xc = xc + cw[kk:kk + 1, :] * xbuf_ref[SUBLANES - shift:SUBLANES - shift + TSL, :]
    xbuf_ref[0:SUBLANES, :] = xbuf_ref[TSL:TSL + SUBLANES, :]

    xcb = xc.astype(BF16)
    r_parts, i_parts = [], []
    for j in range(LRU_HEADS):
        l = slice(j * LRU_BLOCK, (j + 1) * LRU_BLOCK)
        r_parts.append(jnp.dot(xcb[:, l], wra_ref[j], preferred_element_type=F32))
        i_parts.append(jnp.dot(xcb[:, l], wrx_ref[j], preferred_element_type=F32))
    rg = jax.nn.sigmoid(jnp.concatenate(r_parts, axis=1) + bra_ref[...])
    ig = jax.nn.sigmoid(jnp.concatenate(i_parts, axis=1) + brx_ref[...])
    log_a = RG_C * rg * jax.nn.log_sigmoid(lam_ref[...])
    a = jnp.exp(log_a)
    m2 = 1.0 - a * a
    mult = jnp.where(m2 > 0.0, m2 * lax.rsqrt(m2), 0.0)
    is_first = (lax.broadcasted_iota(jnp.int32, (TSL, 1), 0) == 0) & (si == 0)
    mult = jnp.where(is_first, 1.0, mult)
    u = xc * ig * mult
    n_groups = w // LANES
    group_rows = SUBLANES * pitch
    for gi in range(n_groups):
        for j in range(SUBLANES):
            dst = slice(gi * group_rows + j * pitch, gi * group_rows + j * pitch + seg)
            a_ref[dst, :] = a[j * seg:(j + 1) * seg, gi * LANES:(gi + 1) * LANES]
            u_ref[dst, :] = u[j * seg:(j + 1) * seg, gi * LANES:(gi + 1) * LANES]

    def step(i, hp):
        hs, ps = hp
        new_h, new_p = [], []
        for gi in range(n_groups):
            rows = pl.ds(gi * group_rows + i, SUBLANES, stride=pitch)
            ai = a_ref[rows, :]
            h = ai * hs[gi] + u_ref[rows, :]
            p = ai * ps[gi]
            hl_ref[rows, :] = h
            pp_ref[rows, :] = p
            new_h.append(h)
            new_p.append(p)
        return tuple(new_h), tuple(new_p)

    init = (tuple(jnp.zeros((SUBLANES, LANES), F32) for _ in range(n_groups)),
            tuple(jnp.ones((SUBLANES, LANES), F32) for _ in range(n_groups)))
    h_end, p_end = lax.fori_loop(0, seg, step, init, unroll=4)
    h_end = jnp.concatenate(h_end, axis=1)
    p_end = jnp.concatenate(p_end, axis=1)
    c = hc_ref[...]
    h_parts = []
    for j in range(SUBLANES):
        src = [slice(gi * group_rows + j * pitch, gi * group_rows + j * pitch + seg) for gi in range(n_groups)]
        hl = jnp.concatenate([hl_ref[r, :] for r in src], axis=1)
        pp = jnp.concatenate([pp_ref[r, :] for r in src], axis=1)
        h_parts.append(hl + pp * c)
        c = h_end[j:j + 1, :] + p_end[j:j + 1, :] * c
    hc_ref[...] = c
    h_full = jnp.concatenate(h_parts, axis=0)

    gate_b = jnp.dot(hb, wb_ref[:, w:2 * w], preferred_element_type=F32)
    ob = (h_full * jax.nn.gelu(gate_b)).astype(BF16)
    yb = jnp.dot(ob, wpb_ref[...], preferred_element_type=F32)
    gb = jnp.dot(hb, wb_ref[:, 2 * w:3 * w], preferred_element_type=F32)
    merged = ya_ref[0] + jax.nn.sigmoid(gb) * yb
    x1 = x + jnp.dot(merged.astype(BF16), wout_ref[...], preferred_element_type=F32)
    x1_ref[0] = x1

    h2 = _rms(x1, nffn_ref[...])
    h_hi = h2.astype(BF16)
    h2_ref[0] = h_hi
    h_lo = (h2 - h_hi.astype(F32)).astype(BF16)
    wt = wrt_ref[...]
    w_hi = wt.astype(BF16)
    w_lo = (wt - w_hi.astype(F32)).astype(BF16)
    both = lax.dot_general(jnp.concatenate([w_hi, w_lo], axis=0), h_hi, NT_DIMS, preferred_element_type=F32)
    logits = (both[:N_EXPERTS] + both[N_EXPERTS:]
              + lax.dot_general(w_hi, h_lo, NT_DIMS, preferred_element_type=F32)) + brt_ref[...]

    eio = lax.broadcasted_iota(jnp.int32, (N_EXPERTS, TSL), 0)
    idxs, vals = [], []
    for _ in range(TOP_K):
        m = jnp.max(logits, axis=0, keepdims=True)
        ik = jnp.min(jnp.where(logits == m, eio, N_EXPERTS), axis=0, keepdims=True)
        idxs.append(ik)
        vals.append(m)
        logits = jnp.where(eio == ik, -jnp.inf, logits)
    exps = [jnp.exp(v - vals[0]) for v in vals]
    den = functools.reduce(lambda p, q: p + q, exps)
    onehots = [(eio == ik) for ik in idxs]
    oh = functools.reduce(lambda p, q: p + q, [o.astype(F32) for o in onehots])
    cum = jnp.dot(oh.astype(BF16), ustr_ref[...], preferred_element_type=F32) + carry_ref[:, 0:1]
    poss = [jnp.sum(jnp.where(o, cum, 0.0), axis=0, keepdims=True).astype(jnp.int32) for o in onehots]
    carry_ref[...] = carry_ref[...] + jnp.sum(oh, axis=1, keepdims=True)
    ri_ref[...] = jnp.concatenate(idxs + poss, axis=0)
    rw_ref[...] = jnp.concatenate([e / den for e in exps] + [jnp.zeros((SUBLANES - TOP_K, TSL), F32)], axis=0)
    cnt_ref[...] = carry_ref[...]


def _lru_call(x, ya, nw, wb, cw, cb, wra, bra, wrx, brx, lam, wpb, wout, nffn, wrt, brt, ustr):
    b, s, d = x.shape
    w = LRU_WIDTH
    scan_shape = ((TSL + SUBLANES * SUBLANES) * (w // LANES), LANES)
    const = lambda shape: pl.BlockSpec(shape, lambda i, j: (0,) * len(shape))
    tok = pl.BlockSpec((1, TSL, d), lambda i, j: (i, j, 0))
    rt = pl.BlockSpec((SUBLANES, TSL), lambda i, j: (0, i * (s // TSL) + j))
    return pl.pallas_call(
        _lru_kernel,
        grid=(b, s // TSL),
        in_specs=[
            tok, tok, const((1, d)), const((d, 3 * w)), const((CONV_WIDTH, w)), const((1, w)),
            const((LRU_HEADS, LRU_BLOCK, LRU_BLOCK)), const((1, w)),
            const((LRU_HEADS, LRU_BLOCK, LRU_BLOCK)), const((1, w)), const((1, w)),
            const((w, d)), const((d, d)), const((1, d)), const((N_EXPERTS, d)), const((N_EXPERTS, 1)),
            const((TSL, TSL)),
        ],
        out_specs=[tok, tok, rt, rt, const((N_EXPERTS, LANES))],
        out_shape=[
            jax.ShapeDtypeStruct((b, s, d), F32),
            jax.ShapeDtypeStruct((b, s, d), BF16),
            jax.ShapeDtypeStruct((SUBLANES, b * s), jnp.int32),
            jax.ShapeDtypeStruct((SUBLANES, b * s), F32),
            jax.ShapeDtypeStruct((N_EXPERTS, LANES), F32),
        ],
        scratch_shapes=[
            pltpu.VMEM((TSL + SUBLANES, w), F32),
            pltpu.VMEM(scan_shape, F32), pltpu.VMEM(scan_shape, F32),
            pltpu.VMEM(scan_shape, F32), pltpu.VMEM(scan_shape, F32),
            pltpu.VMEM((1, w), F32), pltpu.VMEM((N_EXPERTS, LANES), F32),
        ],
        compiler_params=pltpu.CompilerParams(
            dimension_semantics=("arbitrary", "arbitrary"), vmem_limit_bytes=VMEM_LIMIT),
        name="lru_router",
    )(x, ya, nw, wb, cw, cb, wra, bra, wrx, brx, lam, wpb, wout, nffn, wrt, brt, ustr)


def _invert_slots(slot_flat, n_rows, t):
    n = slot_flat.shape[0]
    vals = jnp.broadcast_to((jnp.arange(n, dtype=jnp.int32) % t)[:, None], (n, SC_ROW))
    mesh = plsc.VectorSubcoreMesh(core_axis_name="core", subcore_axis_name="subcore")

    @functools.partial(pl.kernel, out_type=jax.ShapeDtypeStruct((n_rows, SC_ROW), jnp.int32), mesh=mesh,
                       scratch_types=[])
    def scatter_kernel(x_hbm, i_hbm, o_hbm):
        def body(x_vmem, i_vmem):
            pltpu.sync_copy(x_vmem, o_hbm.at[i_vmem.at[0]])

        pltpu.emit_pipeline(
            body,
            grid=(n // SC_WINDOW,),
            in_specs=[pl.BlockSpec((SC_WINDOW, SC_ROW), lambda i: (i, 0)),
                      pl.BlockSpec((1, SC_WINDOW), lambda i: (0, i))],
            out_specs=[],
            core_axis_name=("core", "subcore"),
            dimension_semantics=(pltpu.PARALLEL,),
        )(x_hbm, i_hbm)

    return scatter_kernel(vals, slot_flat.reshape(1, n))[:, 0]


def _moe_kernel(te_ref, nu_ref, xs_ref, wgu_ref, bg_ref, bl_ref, wd_ref, bd_ref, perm_ref, *rest):
    y_ref, wg_s, wl_s, wd_s = rest[-4:]
    i = pl.program_id(0)
    half = MXU_COLS // 2
    new_expert = (i == 0) | (te_ref[i] != te_ref[jnp.maximum(i - 1, 0)])

    @pl.when(new_expert & (i < nu_ref[0]))
    def _():
        for blk in range(2 * D_FF // MXU_COLS):
            wblk = wgu_ref[0, :, blk * MXU_COLS:(blk + 1) * MXU_COLS].astype(BF16)
            pw = jnp.dot(wblk, perm_ref[...], preferred_element_type=F32)
            wg_s[:, blk * half:(blk + 1) * half] = pw[:, :half].astype(BF16)
            wl_s[:, blk * half:(blk + 1) * half] = pw[:, half:].astype(BF16)
        wd_s[...] = wd_ref[0].astype(BF16)

    @pl.when(i < nu_ref[0])
    def _():
        xs = xs_ref[...]
        g = jnp.dot(xs, wg_s[...], preferred_element_type=F32) + bg_ref[0]
        l = jnp.dot(xs, wl_s[...], preferred_element_type=F32) + bl_ref[0]
        g = jnp.minimum(g, SWIGLU_LIMIT)
        l = jnp.clip(l, -SWIGLU_LIMIT, SWIGLU_LIMIT)
        act = g * jax.nn.sigmoid(SWIGLU_ALPHA * g) * (l + 1.0)
        y = jnp.dot(act.astype(BF16), wd_s[...], preferred_element_type=F32) + bd_ref[0]
        y_ref[...] = y.astype(y_ref.dtype)

    @pl.when(i >= nu_ref[0])
    def _():
        y_ref[...] = jnp.zeros_like(y_ref)


def _moe_call(tile_expert, n_used, xs, wgu, bg, bl, wd, bd, *, y_prev=None, base_tile=0, total_rows=None):
    p, d = xs.shape
    n_tiles = p // TM
    total_rows = p if total_rows is None else total_rows
    tile_expert = lax.slice(tile_expert, (base_tile,), (base_tile + n_tiles,))
    n_used = jnp.clip(n_used - base_tile, 1, n_tiles)
    half = MXU_COLS // 2
    r = lax.broadcasted_iota(jnp.int32, (MXU_COLS, MXU_COLS), 0)
    c = lax.broadcasted_iota(jnp.int32, (MXU_COLS, MXU_COLS), 1)
    perm = (r == jnp.where(c < half, 2 * c, 2 * (c - half) + 1)).astype(BF16)
    row = lambda i, te, nu: (jnp.minimum(i, nu[0] - 1), 0)
    ex3 = lambda i, te, nu: (te[i], 0, 0)
    in_specs = [
        pl.BlockSpec((TM, d), row),
        pl.BlockSpec((1, d, 2 * D_FF), ex3),
        pl.BlockSpec((1, 1, D_FF), ex3), pl.BlockSpec((1, 1, D_FF), ex3),
        pl.BlockSpec((1, D_FF, d), ex3), pl.BlockSpec((1, 1, d), ex3),
        pl.BlockSpec((MXU_COLS, MXU_COLS), lambda i, te, nu: (0, 0)),
    ]
    operands = [tile_expert, n_used, xs, wgu, bg, bl, wd, bd, perm]
    aliases = {}
    if y_prev is not None:
        in_specs.append(pl.BlockSpec(memory_space=pl.ANY))
        aliases = {len(operands): 0}
        operands.append(y_prev)
    grid_spec = pltpu.PrefetchScalarGridSpec(
        num_scalar_prefetch=2,
        grid=(n_tiles,),
        in_specs=in_specs,
        out_specs=pl.BlockSpec((TM, d), lambda i, te, nu: (i + base_tile, 0)),
        scratch_shapes=[pltpu.VMEM((d, D_FF), BF16), pltpu.VMEM((d, D_FF), BF16), pltpu.VMEM((D_FF, d), BF16)],
    )
    return pl.pallas_call(
        _moe_kernel,
        grid_spec=grid_spec,
        out_shape=jax.ShapeDtypeStruct((total_rows, d), BF16),
        input_output_aliases=aliases,
        compiler_params=pltpu.CompilerParams(
            dimension_semantics=("arbitrary",), vmem_limit_bytes=VMEM_LIMIT),
        name="moe_experts",
    )(*operands)


def _final_kernel(x1_ref, yg_ref, gw_ref, nw_ref, *rest):
    out_ref = rest[-1]
    gw = gw_ref[...]
    x2 = x1_ref[...]
    for k in range(TOP_K):
        x2 = x2 + gw[:, k:k + 1] * yg_ref[k].astype(F32)
    out_ref[...] = _rms(x2, nw_ref[...])


def _final_call(x1, yg, gw, nw, *, out_prev=None, base_block=0):
    t, d = x1.shape
    blk = pl.BlockSpec((TF, d), lambda i: (i + base_block, 0))
    in_specs = [blk, pl.BlockSpec((TOP_K, TF, d), lambda i: (0, i, 0)),
                pl.BlockSpec((TF, TOP_K), lambda i: (i + base_block, 0)), pl.BlockSpec((1, d), lambda i: (0, 0))]
    operands = [x1, yg, gw, nw]
    aliases = {}
    if out_prev is not None:
        in_specs.append(pl.BlockSpec(memory_space=pl.ANY))
        aliases = {len(operands): 0}
        operands.append(out_prev)
    return pl.pallas_call(
        _final_kernel,
        grid=(yg.shape[1] // TF,),
        in_specs=in_specs,
        out_specs=blk,
        out_shape=jax.ShapeDtypeStruct((t, d), F32),
        input_output_aliases=aliases,
        compiler_params=pltpu.CompilerParams(dimension_semantics=("arbitrary",)),
        name="final_norm",
    )(*operands)


def _chunk_masks():
    r = lax.broadcasted_iota(jnp.int32, (TSH, TSH), 0)
    c = lax.broadcasted_iota(jnp.int32, (TSH, TSH), 1)
    tri = ((r // CHUNK == c // CHUNK) & (r >= c)).astype(BF16)
    r = lax.broadcasted_iota(jnp.int32, (TSL, TSL), 0)
    c = lax.broadcasted_iota(jnp.int32, (TSL, TSL), 1)
    ustr = (r < c).astype(BF16)
    return tri, ustr


def kernel(x, norm_mix, w_in, lb_raw, gn_w, conv_w, conv_b, w_rg_a, b_rg_a, w_rg_x, b_rg_x, lru_lambda,
           w_proj_a, w_proj_b, w_out, norm_ffn, w_router, b_router, w_gu, b_gu, w_down, b_down, norm_final):
    b, s, d = x.shape
    t = b * s
    assert w_in.shape[0] == 1 and s % TSH == 0 and s % TSL == 0
    hk = HA_HEADS * HA_DK
    w = LRU_WIDTH
    tri, ustr = _chunk_masks()

    wi = w_in[0]
    o_lru = 4 * hk
    wa = wi[:, 0:o_lru].astype(BF16)
    wb = jnp.concatenate([wi[:, o_lru:o_lru + 2 * w], wi[:, o_lru + 2 * w + d:o_lru + 2 * w + 2 * d]],
                         axis=1).astype(BF16)
    wga = wi[:, o_lru + 2 * w:o_lru + 2 * w + d].astype(BF16)
    row = lambda v: v.reshape(1, -1)

    ya = _hgrn_call(x, row(norm_mix[0]), wa, wga, lb_raw, row(gn_w[0]), w_proj_a[0].astype(BF16), tri)
    x1, h2, ri, rw, cnt = _lru_call(
        x, ya, row(norm_mix[0]), wb, conv_w[0], row(conv_b[0]), w_rg_a[0].astype(BF16), row(b_rg_a[0]),
        w_rg_x[0].astype(BF16), row(b_rg_x[0]), row(lru_lambda[0]), w_proj_b[0].astype(BF16),
        w_out[0].astype(BF16), row(norm_ffn[0]), w_router[0].T, b_router[0].reshape(-1, 1), ustr)

    idx, pos, gate_w = ri[0:TOP_K], ri[TOP_K:2 * TOP_K], rw[0:TOP_K]
    counts = cnt[:, 0].astype(jnp.int32)
    padded = ((counts + TM - 1) // TM) * TM
    ends = jnp.cumsum(padded)
    offs = ends - padded
    idx_d, slot = idx.reshape(-1, LANES), pos.reshape(-1, LANES)
    for e in range(N_EXPERTS):
        slot = slot + jnp.where(idx_d == e, offs[e], 0)
    slot = slot.reshape(TOP_K, t)
    n_rows = TOP_K * t + N_EXPERTS * TM
    n_tiles = n_rows // TM
    tile_expert = jnp.minimum(
        jnp.sum(jnp.arange(n_tiles, dtype=jnp.int32)[:, None] * TM >= ends[None, :], axis=1),
        N_EXPERTS - 1).astype(jnp.int32)
    n_used = (ends[-1:] // TM).astype(jnp.int32)

    n_pad = padded - counts
    pads_before = jnp.cumsum(n_pad) - n_pad
    j = jnp.arange(TM, dtype=jnp.int32)[None, :]
    e_base = (jnp.arange(N_EXPERTS, dtype=jnp.int32) * TM)[:, None]
    pad_slot = jnp.where(j < n_pad[:, None], (offs + counts)[:, None] + j,
                         ends[-1] + e_base + j - (pads_before + n_pad)[:, None])
    all_slots = jnp.concatenate([slot.reshape(-1), pad_slot.reshape(-1)])
    tok_of_slot = _invert_slots(all_slots, n_rows, t)
    def bounds(total, unit, parts):
        cuts = [total * sum(parts[:k]) // sum(parts) for k in range(len(parts) + 1)]
        assert all(c % unit == 0 for c in cuts)
        return list(zip(cuts[:-1], cuts[1:]))

    h2r, x1r = h2.reshape(t, d), x1.reshape(t, d)
    bgs, bls = b_gu[0][:, 0::2].reshape(N_EXPERTS, 1, D_FF), b_gu[0][:, 1::2].reshape(N_EXPERTS, 1, D_FF)
    bds = b_down[0].reshape(N_EXPERTS, 1, d)
    row_parts = bounds(n_rows, TM, MOE_PARTS)
    assert row_parts[-1][0] < TOP_K * t
    y = None
    for lo, hi in row_parts:
        xs = h2r.at[tok_of_slot[lo:hi]].get(mode="promise_in_bounds")
        y = _moe_call(tile_expert, n_used, xs, w_gu[0], bgs, bls, w_down[0], bds,
                      y_prev=y, base_tile=lo // TM, total_rows=n_rows)

    gw = gate_w.T
    out = None
    for lo, hi in bounds(t, TF, FINAL_PARTS):
        yg = y.at[slot[:, lo:hi]].get(mode="promise_in_bounds")
        out = _final_call(x1r, yg, gw, row(norm_final), out_prev=out, base_block=lo // TF)
    return out.reshape(b, s, d)
```

```python
import functools

import jax
import jax.numpy as jnp
from jax import lax
from jax.experimental import pallas as pl
from jax.experimental.pallas import tpu as pltpu
from jax.experimental.pallas import tpu_sc as plsc

F32 = jnp.float32
BF16 = jnp.bfloat16

D_MODEL = 1024
HA_HEADS = 8
HA_DK = 128
HA_DV = 128
CHUNK = 64
LRU_WIDTH = 1024
LRU_HEADS = 4
LRU_BLOCK = LRU_WIDTH // LRU_HEADS
CONV_WIDTH = 4
RG_C = 8.0
N_EXPERTS = 32
TOP_K = 4
D_FF = 1024
SWIGLU_LIMIT = 7.0
SWIGLU_ALPHA = 1.702
EPS = 1e-5

SUBLANES = 8
LANES = 128
MXU_COLS = 256
TSH = 256
TSL = 512
TM = 512
TF = 512
MOE_PARTS = (1, 3)
FINAL_PARTS = (1,) * 8
VMEM_LIMIT = 56 * 1024 * 1024
SC_WINDOW = 128
SC_ROW = 128

NT_DIMS = (((1,), (1,)), ((), ()))
TN_DIMS = (((0,), (0,)), ((), ()))


def _rms(x, w):
    return x * lax.rsqrt(jnp.mean(x * x, axis=-1, keepdims=True) + EPS) * w


def _split3(v):
    hi = v.astype(BF16)
    r1 = v - hi.astype(F32)
    mid = r1.astype(BF16)
    lo = (r1 - mid.astype(F32)).astype(BF16)
    return hi, mid, lo


def _hgrn_kernel(x_ref, nw_ref, wa_ref, wga_ref, lbraw_ref, gnw_ref, wpa_ref, tri_ref, out_ref,
                 st_ref, qin_ref, kin_ref, kdec_ref, v_ref, g_ref, oa_ref):
    n_chunks = TSH // CHUNK
    hk = HA_HEADS * HA_DK

    @pl.when(pl.program_id(1) == 0)
    def _():
        st_ref[...] = jnp.zeros_like(st_ref)

    hb = _rms(x_ref[0], nw_ref[...]).astype(BF16)

    lbr = lbraw_ref[...]
    rows = [lbr[j:j + 1, :] for j in range(lbr.shape[0])]
    mx = functools.reduce(jnp.maximum, rows)
    es = [jnp.exp(r - mx) for r in rows]
    lb = es[0] / functools.reduce(lambda a, b: a + b, es)

    fl = jnp.dot(hb, wa_ref[:, hk:2 * hk], preferred_element_type=F32)
    f = lb + (1.0 - lb) * jax.nn.sigmoid(fl)
    lf = jnp.log(f)
    k = 1.0 - f
    tri = tri_ref[...]
    bcum = functools.reduce(lambda a, b: a + b,
                            [jnp.dot(tri, p, preferred_element_type=F32) for p in _split3(lf)])
    b_last = [bcum[c * CHUNK + CHUNK - 1:c * CHUNK + CHUNK, :] for c in range(n_chunks)]
    bl_full = jnp.concatenate([jnp.broadcast_to(b, (CHUNK, hk)) for b in b_last], axis=0)

    q = jnp.dot(hb, wa_ref[:, 0:hk], preferred_element_type=F32)
    qin_ref[...] = (jax.nn.silu(q) * jnp.exp(bcum)).astype(BF16)
    kin_ref[...] = (k * jnp.exp(-bcum)).astype(BF16)
    kdec_ref[...] = (k * jnp.exp(bl_full - bcum)).astype(BF16)
    v_ref[...] = jnp.dot(hb, wa_ref[:, 2 * hk:3 * hk], preferred_element_type=F32).astype(BF16)
    g = jnp.dot(hb, wa_ref[:, 3 * hk:4 * hk], preferred_element_type=F32)
    g_ref[...] = jax.nn.silu(g) * gnw_ref[...]

    causal = (lax.broadcasted_iota(jnp.int32, (CHUNK, CHUNK), 0)
              >= lax.broadcasted_iota(jnp.int32, (CHUNK, CHUNK), 1))
    sts = [st_ref[h] for h in range(HA_HEADS)]
    for c in range(n_chunks):
        r = slice(c * CHUNK, (c + 1) * CHUNK)
        edec = jnp.exp(b_last[c])
        for h in range(HA_HEADS):
            l = slice(h * HA_DK, (h + 1) * HA_DK)
            qh, kh, kd, vh = qin_ref[r, l], kin_ref[r, l], kdec_ref[r, l], v_ref[r, l]
            s = lax.dot_general(qh, kh, NT_DIMS, preferred_element_type=F32)
            s = jnp.where(causal, s, 0.0).astype(BF16)
            o = (jnp.dot(s, vh, preferred_element_type=F32)
                 + jnp.dot(qh, sts[h].T.astype(BF16), preferred_element_type=F32))
            sts[h] = sts[h] * edec[:, l] + lax.dot_general(vh, kd, TN_DIMS, preferred_element_type=F32)
            ms = jnp.mean(o * o, axis=-1, keepdims=True)
            oa_ref[r, l] = (o * lax.rsqrt(ms + EPS) * g_ref[r, l]).astype(BF16)
    for h in range(HA_HEADS):
        st_ref[h] = sts[h]

    ya = jnp.dot(oa_ref[...], wpa_ref[...], preferred_element_type=F32)
    ga = jnp.dot(hb, wga_ref[...], preferred_element_type=F32)
    out_ref[0] = jax.nn.sigmoid(ga) * ya


def _hgrn_call(x, nw, wa, wga, lb_raw, gnw, wpa, tri):
    b, s, d = x.shape
    hk = HA_HEADS * HA_DK
    const = lambda shape: pl.BlockSpec(shape, lambda i, j: (0,) * len(shape))
    return pl.pallas_call(
        _hgrn_kernel,
        grid=(b, s // TSH),
        in_specs=[
            pl.BlockSpec((1, TSH, d), lambda i, j: (i, j, 0)),
            const((1, d)), const((d, 4 * hk)), const((d, d)), const(lb_raw.shape), const((1, hk)),
            const((hk, d)), const((TSH, TSH)),
        ],
        out_specs=pl.BlockSpec((1, TSH, d), lambda i, j: (i, j, 0)),
        out_shape=jax.ShapeDtypeStruct((b, s, d), F32),
        scratch_shapes=[
            pltpu.VMEM((HA_HEADS, HA_DV, HA_DK), F32),
            pltpu.VMEM((TSH, hk), BF16), pltpu.VMEM((TSH, hk), BF16), pltpu.VMEM((TSH, hk), BF16),
            pltpu.VMEM((TSH, hk), BF16), pltpu.VMEM((TSH, hk), F32), pltpu.VMEM((TSH, hk), BF16),
        ],
        compiler_params=pltpu.CompilerParams(
            dimension_semantics=("arbitrary", "arbitrary"), vmem_limit_bytes=VMEM_LIMIT),
        name="hgrn",
    )(x, nw, wa, wga, lb_raw, gnw, wpa, tri)


def _lru_kernel(x_ref, ya_ref, nw_ref, wb_ref, cw_ref, cb_ref, wra_ref, bra_ref, wrx_ref, brx_ref, lam_ref,
                wpb_ref, wout_ref, nffn_ref, wrt_ref, brt_ref, ustr_ref,
                x1_ref, h2_ref, ri_ref, rw_ref, cnt_ref,
                xbuf_ref, a_ref, u_ref, hl_ref, pp_ref, hc_ref, carry_ref):
    seg = TSL // SUBLANES
    pitch = seg + SUBLANES
    w = LRU_WIDTH
    bi, si = pl.program_id(0), pl.program_id(1)

    @pl.when(si == 0)
    def _():
        xbuf_ref[0:SUBLANES, :] = jnp.zeros((SUBLANES, w), F32)
        hc_ref[...] = jnp.zeros_like(hc_ref)

    @pl.when((bi == 0) & (si == 0))
    def _():
        carry_ref[...] = jnp.zeros_like(carry_ref)

    x = x_ref[0]
    hb = _rms(x, nw_ref[...]).astype(BF16)

    xb = jnp.dot(hb, wb_ref[:, 0:w], preferred_element_type=F32)
    xbuf_ref[SUBLANES:SUBLANES + TSL, :] = xb
    cw = cw_ref[...]
    xc = cb_ref[...] + cw[CONV_WIDTH - 1:CONV_WIDTH, :] * xb
    for kk in range(CONV_WIDTH - 1):
        shift = CONV_WIDTH - 1 - kk
        xc = xc + cw[kk:kk + 1, :] * xbuf_ref[SUBLANES - shift:SUBLANES - shift + TSL, :]
    xbuf_ref[0:SUBLANES, :] = xbuf_ref[TSL:TSL + SUBLANES, :]

    xcb = xc.astype(BF16)
    r_parts, i_parts = [], []
    for j in range(LRU_HEADS):
        l = slice(j * LRU_BLOCK, (j + 1) * LRU_BLOCK)
        r_parts.append(jnp.dot(xcb[:, l], wra_ref[j], preferred_element_type=F32))
        i_parts.append(jnp.dot(xcb[:, l], wrx_ref[j], preferred_element_type=F32))
    rg = jax.nn.sigmoid(jnp.concatenate(r_parts, axis=1) + bra_ref[...])
    ig = jax.nn.sigmoid(jnp.concatenate(i_parts, axis=1) + brx_ref[...])
    log_a = RG_C * rg * jax.nn.log_sigmoid(lam_ref[...])
    a = jnp.exp(log_a)
    m2 = 1.0 - a * a
    mult = jnp.where(m2 > 0.0, m2 * lax.rsqrt(m2), 0.0)
    is_first = (lax.broadcasted_iota(jnp.int32, (TSL, 1), 0) == 0) & (si == 0)
    mult = jnp.where(is_first, 1.0, mult)
    u = xc * ig * mult
    n_groups = w // LANES
    group_rows = SUBLANES * pitch
    for gi in range(n_groups):
        for j in range(SUBLANES):
            dst = slice(gi * group_rows + j * pitch, gi * group_rows + j * pitch + seg)
            a_ref[dst, :] = a[j * seg:(j + 1) * seg, gi * LANES:(gi + 1) * LANES]
            u_ref[dst, :] = u[j * seg:(j + 1) * seg, gi * LANES:(gi + 1) * LANES]

    def step(i, hp):
        hs, ps = hp
        new_h, new_p = [], []
        for gi in range(n_groups):
            rows = pl.ds(gi * group_rows + i, SUBLANES, stride=pitch)
            ai = a_ref[rows, :]
            h = ai * hs[gi] + u_ref[rows, :]
            p = ai * ps[gi]
            hl_ref[rows, :] = h
            pp_ref[rows, :] = p
            new_h.append(h)
            new_p.append(p)
        return tuple(new_h), tuple(new_p)

    init = (tuple(jnp.zeros((SUBLANES, LANES), F32) for _ in range(n_groups)),
            tuple(jnp.ones((SUBLANES, LANES), F32) for _ in range(n_groups)))
    h_end, p_end = lax.fori_loop(0, seg, step, init, unroll=4)
    h_end = jnp.concatenate(h_end, axis=1)
    p_end = jnp.concatenate(p_end, axis=1)
    c = hc_ref[...]
    h_parts = []
    for j in range(SUBLANES):
        src = [slice(gi * group_rows + j * pitch, gi * group_rows + j * pitch + seg) for gi in range(n_groups)]
        hl = jnp.concatenate([hl_ref[r, :] for r in src], axis=1)
        pp = jnp.concatenate([pp_ref[r, :] for r in src], axis=1)
        h_parts.append(hl + pp * c)
        c = h_end[j:j + 1, :] + p_end[j:j + 1, :] * c
    hc_ref[...] = c
    h_full = jnp.concatenate(h_parts, axis=0)

    gate_b = jnp.dot(hb, wb_ref[:, w:2 * w], preferred_element_type=F32)
    ob = (h_full * jax.nn.gelu(gate_b)).astype(BF16)
    yb = jnp.dot(ob, wpb_ref[...], preferred_element_type=F32)
    gb = jnp.dot(hb, wb_ref[:, 2 * w:3 * w], preferred_element_type=F32)
    merged = ya_ref[0] + jax.nn.sigmoid(gb) * yb
    x1 = x + jnp.dot(merged.astype(BF16), wout_ref[...], preferred_element_type=F32)
    x1_ref[0] = x1

    h2 = _rms(x1, nffn_ref[...])
    h_hi = h2.astype(BF16)
    h2_ref[0] = h_hi
    h_lo = (h2 - h_hi.astype(F32)).astype(BF16)
    wt = wrt_ref[...]
    w_hi = wt.astype(BF16)
    w_lo = (wt - w_hi.astype(F32)).astype(BF16)
    both = lax.dot_general(jnp.concatenate([w_hi, w_lo], axis=0), h_hi, NT_DIMS, preferred_element_type=F32)
    logits = (both[:N_EXPERTS] + both[N_EXPERTS:]
              + lax.dot_general(w_hi, h_lo, NT_DIMS, preferred_element_type=F32)) + brt_ref[...]

    eio = lax.broadcasted_iota(jnp.int32, (N_EXPERTS, TSL), 0)
    idxs, vals = [], []
    for _ in range(TOP_K):
        m = jnp.max(logits, axis=0, keepdims=True)
        ik = jnp.min(jnp.where(logits == m, eio, N_EXPERTS), axis=0, keepdims=True)
        idxs.append(ik)
        vals.append(m)
        logits = jnp.where(eio == ik, -jnp.inf, logits)
    exps = [jnp.exp(v - vals[0]) for v in vals]
    den = functools.reduce(lambda p, q: p + q, exps)
    onehots = [(eio == ik) for ik in idxs]
    oh = functools.reduce(lambda p, q: p + q, [o.astype(F32) for o in onehots])
    cum = jnp.dot(oh.astype(BF16), ustr_ref[...], preferred_element_type=F32) + carry_ref[:, 0:1]
    poss = [jnp.sum(jnp.where(o, cum, 0.0), axis=0, keepdims=True).astype(jnp.int32) for o in onehots]
    carry_ref[...] = carry_ref[...] + jnp.sum(oh, axis=1, keepdims=True)
    ri_ref[...] = jnp.concatenate(idxs + poss, axis=0)
    rw_ref[...] = jnp.concatenate([e / den for e in exps] + [jnp.zeros((SUBLANES - TOP_K, TSL), F32)], axis=0)
    cnt_ref[...] = carry_ref[...]


def _lru_call(x, ya, nw, wb, cw, cb, wra, bra, wrx, brx, lam, wpb, wout, nffn, wrt, brt, ustr):
    b, s, d = x.shape
    w = LRU_WIDTH
    scan_shape = ((TSL + SUBLANES * SUBLANES) * (w // LANES), LANES)
    const = lambda shape: pl.BlockSpec(shape, lambda i, j: (0,) * len(shape))
    tok = pl.BlockSpec((1, TSL, d), lambda i, j: (i, j, 0))
    rt = pl.BlockSpec((SUBLANES, TSL), lambda i, j: (0, i * (s // TSL) + j))
    return pl.pallas_call(
        _lru_kernel,
        grid=(b, s // TSL),
        in_specs=[
            tok, tok, const((1, d)), const((d, 3 * w)), const((CONV_WIDTH, w)), const((1, w)),
            const((LRU_HEADS, LRU_BLOCK, LRU_BLOCK)), const((1, w)),
            const((LRU_HEADS, LRU_BLOCK, LRU_BLOCK)), const((1, w)), const((1, w)),
            const((w, d)), const((d, d)), const((1, d)), const((N_EXPERTS, d)), const((N_EXPERTS, 1)),
            const((TSL, TSL)),
        ],
        out_specs=[tok, tok, rt, rt, const((N_EXPERTS, LANES))],
        out_shape=[
            jax.ShapeDtypeStruct((b, s, d), F32),
            jax.ShapeDtypeStruct((b, s, d), BF16),
            jax.ShapeDtypeStruct((SUBLANES, b * s), jnp.int32),
            jax.ShapeDtypeStruct((SUBLANES, b * s), F32),
            jax.ShapeDtypeStruct((N_EXPERTS, LANES), F32),
        ],
        scratch_shapes=[
            pltpu.VMEM((TSL + SUBLANES, w), F32),
            pltpu.VMEM(scan_shape, F32), pltpu.VMEM(scan_shape, F32),
            pltpu.VMEM(scan_shape, F32), pltpu.VMEM(scan_shape, F32),
            pltpu.VMEM((1, w), F32), pltpu.VMEM((N_EXPERTS, LANES), F32),
        ],
        compiler_params=pltpu.CompilerParams(
            dimension_semantics=("arbitrary", "arbitrary"), vmem_limit_bytes=VMEM_LIMIT),
        name="lru_router",
    )(x, ya, nw, wb, cw, cb, wra, bra, wrx, brx, lam, wpb, wout, nffn, wrt, brt, ustr)


def _invert_slots(slot_flat, n_rows, t):
    n = slot_flat.shape[0]
    vals = jnp.broadcast_to((jnp.arange(n, dtype=jnp.int32) % t)[:, None], (n, SC_ROW))
    mesh = plsc.VectorSubcoreMesh(core_axis_name="core", subcore_axis_name="subcore")

    @functools.partial(pl.kernel, out_type=jax.ShapeDtypeStruct((n_rows, SC_ROW), jnp.int32), mesh=mesh,
                       scratch_types=[])
    def scatter_kernel(x_hbm, i_hbm, o_hbm):
        def body(x_vmem, i_vmem):
            pltpu.sync_copy(x_vmem, o_hbm.at[i_vmem.at[0]])

        pltpu.emit_pipeline(
            body,
            grid=(n // SC_WINDOW,),
            in_specs=[pl.BlockSpec((SC_WINDOW, SC_ROW), lambda i: (i, 0)),
                      pl.BlockSpec((1, SC_WINDOW), lambda i: (0, i))],
            out_specs=[],
            core_axis_name=("core", "subcore"),
            dimension_semantics=(pltpu.PARALLEL,),
        )(x_hbm, i_hbm)

    return scatter_kernel(vals, slot_flat.reshape(1, n))[:, 0]


def _moe_kernel(te_ref, nu_ref, xs_ref, wgu_ref, bg_ref, bl_ref, wd_ref, bd_ref, perm_ref, *rest):
    y_ref, wg_s, wl_s, wd_s = rest[-4:]
    i = pl.program_id(0)
    half = MXU_COLS // 2
    new_expert = (i == 0) | (te_ref[i] != te_ref[jnp.maximum(i - 1, 0)])

    @pl.when(new_expert & (i < nu_ref[0]))
    def _():
        for blk in range(2 * D_FF // MXU_COLS):
            wblk = wgu_ref[0, :, blk * MXU_COLS:(blk + 1) * MXU_COLS].astype(BF16)
            pw = jnp.dot(wblk, perm_ref[...], preferred_element_type=F32)
            wg_s[:, blk * half:(blk + 1) * half] = pw[:, :half].astype(BF16)
            wl_s[:, blk * half:(blk + 1) * half] = pw[:, half:].astype(BF16)
        wd_s[...] = wd_ref[0].astype(BF16)

    @pl.when(i < nu_ref[0])
    def _():
        xs = xs_ref[...]
        g = jnp.dot(xs, wg_s[...], preferred_element_type=F32) + bg_ref[0]
        l = jnp.dot(xs, wl_s[...], preferred_element_type=F32) + bl_ref[0]
        g = jnp.minimum(g, SWIGLU_LIMIT)
        l = jnp.clip(l, -SWIGLU_LIMIT, SWIGLU_LIMIT)
        act = g * jax.nn.sigmoid(SWIGLU_ALPHA * g) * (l + 1.0)
        y = jnp.dot(act.astype(BF16), wd_s[...], preferred_element_type=F32) + bd_ref[0]
        y_ref[...] = y.astype(y_ref.dtype)

    @pl.when(i >= nu_ref[0])
    def _():
        y_ref[...] = jnp.zeros_like(y_ref)


def _moe_call(tile_expert, n_used, xs, wgu, bg, bl, wd, bd, *, y_prev=None, base_tile=0, total_rows=None):
    p, d = xs.shape
    n_tiles = p // TM
    total_rows = p if total_rows is None else total_rows
    tile_expert = lax.slice(tile_expert, (base_tile,), (base_tile + n_tiles,))
    n_used = jnp.clip(n_used - base_tile, 1, n_tiles)
    half = MXU_COLS // 2
    r = lax.broadcasted_iota(jnp.int32, (MXU_COLS, MXU_COLS), 0)
    c = lax.broadcasted_iota(jnp.int32, (MXU_COLS, MXU_COLS), 1)
    perm = (r == jnp.where(c < half, 2 * c, 2 * (c - half) + 1)).astype(BF16)
    row = lambda i, te, nu: (jnp.minimum(i, nu[0] - 1), 0)
    ex3 = lambda i, te, nu: (te[i], 0, 0)
    in_specs = [
        pl.BlockSpec((TM, d), row),
        pl.BlockSpec((1, d, 2 * D_FF), ex3),
        pl.BlockSpec((1, 1, D_FF), ex3), pl.BlockSpec((1, 1, D_FF), ex3),
        pl.BlockSpec((1, D_FF, d), ex3), pl.BlockSpec((1, 1, d), ex3),
        pl.BlockSpec((MXU_COLS, MXU_COLS), lambda i, te, nu: (0, 0)),
    ]
    operands = [tile_expert, n_used, xs, wgu, bg, bl, wd, bd, perm]
    aliases = {}
    if y_prev is not None:
        in_specs.append(pl.BlockSpec(memory_space=pl.ANY))
        aliases = {len(operands): 0}
        operands.append(y_prev)
    grid_spec = pltpu.PrefetchScalarGridSpec(
        num_scalar_prefetch=2,
        grid=(n_tiles,),
        in_specs=in_specs,
        out_specs=pl.BlockSpec((TM, d), lambda i, te, nu: (i + base_tile, 0)),
        scratch_shapes=[pltpu.VMEM((d, D_FF), BF16), pltpu.VMEM((d, D_FF), BF16), pltpu.VMEM((D_FF, d), BF16)],
    )
    return pl.pallas_call(
        _moe_kernel,
        grid_spec=grid_spec,
        out_shape=jax.ShapeDtypeStruct((total_rows, d), BF16),
        input_output_aliases=aliases,
        compiler_params=pltpu.CompilerParams(
            dimension_semantics=("arbitrary",), vmem_limit_bytes=VMEM_LIMIT),
        name="moe_experts",
    )(*operands)


def _final_kernel(x1_ref, yg_ref, gw_ref, nw_ref, *rest):
    out_ref = rest[-1]
    gw = gw_ref[...]
    x2 = x1_ref[...]
    for k in range(TOP_K):
        x2 = x2 + gw[:, k:k + 1] * yg_ref[k].astype(F32)
    out_ref[...] = _rms(x2, nw_ref[...])


def _final_call(x1, yg, gw, nw, *, out_prev=None, base_block=0):
    t, d = x1.shape
    blk = pl.BlockSpec((TF, d), lambda i: (i + base_block, 0))
    in_specs = [blk, pl.BlockSpec((TOP_K, TF, d), lambda i: (0, i, 0)),
                pl.BlockSpec((TF, TOP_K), lambda i: (i + base_block, 0)), pl.BlockSpec((1, d), lambda i: (0, 0))]
    operands = [x1, yg, gw, nw]
    aliases = {}
    if out_prev is not None:
        in_specs.append(pl.BlockSpec(memory_space=pl.ANY))
        aliases = {len(operands): 0}
        operands.append(out_prev)
    return pl.pallas_call(
        _final_kernel,
        grid=(yg.shape[1] // TF,),
        in_specs=in_specs,
        out_specs=blk,
        out_shape=jax.ShapeDtypeStruct((t, d), F32),
        input_output_aliases=aliases,
        compiler_params=pltpu.CompilerParams(dimension_semantics=("arbitrary",)),
        name="final_norm",
    )(*operands)


def _chunk_masks():
    r = lax.broadcasted_iota(jnp.int32, (TSH, TSH), 0)
    c = lax.broadcasted_iota(jnp.int32, (TSH, TSH), 1)
    tri = ((r // CHUNK == c // CHUNK) & (r >= c)).astype(BF16)
    r = lax.broadcasted_iota(jnp.int32, (TSL, TSL), 0)
    c = lax.broadcasted_iota(jnp.int32, (TSL, TSL), 1)
    ustr = (r < c).astype(BF16)
    return tri, ustr


def kernel(x, norm_mix, w_in, lb_raw, gn_w, conv_w, conv_b, w_rg_a, b_rg_a, w_rg_x, b_rg_x, lru_lambda,
           w_proj_a, w_proj_b, w_out, norm_ffn, w_router, b_router, w_gu, b_gu, w_down, b_down, norm_final):
    b, s, d = x.shape
    t = b * s
    assert w_in.shape[0] == 1 and s % TSH == 0 and s % TSL == 0
    hk = HA_HEADS * HA_DK
    w = LRU_WIDTH
    tri, ustr = _chunk_masks()

    wi = w_in[0]
    o_lru = 4 * hk
    wa = wi[:, 0:o_lru].astype(BF16)
    wb = jnp.concatenate([wi[:, o_lru:o_lru + 2 * w], wi[:, o_lru + 2 * w + d:o_lru + 2 * w + 2 * d]],
                         axis=1).astype(BF16)
    wga = wi[:, o_lru + 2 * w:o_lru + 2 * w + d].astype(BF16)
    row = lambda v: v.reshape(1, -1)

    ya = _hgrn_call(x, row(norm_mix[0]), wa, wga, lb_raw, row(gn_w[0]), w_proj_a[0].astype(BF16), tri)
    x1, h2, ri, rw, cnt = _lru_call(
        x, ya, row(norm_mix[0]), wb, conv_w[0], row(conv_b[0]), w_rg_a[0].astype(BF16), row(b_rg_a[0]),
        w_rg_x[0].astype(BF16), row(b_rg_x[0]), row(lru_lambda[0]), w_proj_b[0].astype(BF16),
        w_out[0].astype(BF16), row(norm_ffn[0]), w_router[0].T, b_router[0].reshape(-1, 1), ustr)

    idx, pos, gate_w = ri[0:TOP_K], ri[TOP_K:2 * TOP_K], rw[0:TOP_K]
    counts = cnt[:, 0].astype(jnp.int32)
    padded = ((counts + TM - 1) // TM) * TM
    ends = jnp.cumsum(padded)
    offs = ends - padded
    idx_d, slot = idx.reshape(-1, LANES), pos.reshape(-1, LANES)
    for e in range(N_EXPERTS):
        slot = slot + jnp.where(idx_d == e, offs[e], 0)
    slot = slot.reshape(TOP_K, t)
    n_rows = TOP_K * t + N_EXPERTS * TM
    n_tiles = n_rows // TM
    tile_expert = jnp.minimum(
        jnp.sum(jnp.arange(n_tiles, dtype=jnp.int32)[:, None] * TM >= ends[None, :], axis=1),
        N_EXPERTS - 1).astype(jnp.int32)
    n_used = (ends[-1:] // TM).astype(jnp.int32)

    n_pad = padded - counts
    pads_before = jnp.cumsum(n_pad) - n_pad
    j = jnp.arange(TM, dtype=jnp.int32)[None, :]
    e_base = (jnp.arange(N_EXPERTS, dtype=jnp.int32) * TM)[:, None]
    pad_slot = jnp.where(j < n_pad[:, None], (offs + counts)[:, None] + j,
                         ends[-1] + e_base + j - (pads_before + n_pad)[:, None])
    all_slots = jnp.concatenate([slot.reshape(-1), pad_slot.reshape(-1)])
    tok_of_slot = _invert_slots(all_slots, n_rows, t)
    def bounds(total, unit, parts):
        cuts = [total * sum(parts[:k]) // sum(parts) for k in range(len(parts) + 1)]
        assert all(c % unit == 0 for c in cuts)
        return list(zip(cuts[:-1], cuts[1:]))

    h2r, x1r = h2.reshape(t, d), x1.reshape(t, d)
    bgs, bls = b_gu[0][:, 0::2].reshape(N_EXPERTS, 1, D_FF), b_gu[0][:, 1::2].reshape(N_EXPERTS, 1, D_FF)
    bds = b_down[0].reshape(N_EXPERTS, 1, d)
    row_parts = bounds(n_rows, TM, MOE_PARTS)
    assert row_parts[-1][0] < TOP_K * t
    y = None
    for lo, hi in row_parts:
        xs = h2r.at[tok_of_slot[lo:hi]].get(mode="promise_in_bounds")
        y = _moe_call(tile_expert, n_used, xs, w_gu[0], bgs, bls, w_down[0], bds,
                      y_prev=y, base_tile=lo // TM, total_rows=n_rows)

    gw = gate_w.T
    out = None
    for lo, hi in bounds(t, TF, FINAL_PARTS):
        yg = y.at[slot[:, lo:hi]].get(mode="promise_in_bounds")
        out = _final_call(x1r, yg, gw, row(norm_final), out_prev=out, base_block=lo // TF)
    return out.reshape(b, s, d)
```

```python
import functools

import jax
import jax.numpy as jnp
from jax import lax
from jax.experimental import pallas as pl
from jax.experimental.pallas import tpu as pltpu
from jax.experimental.pallas import tpu_sc as plsc

F32 = jnp.float32
BF16 = jnp.bfloat16

D_MODEL = 1024
HA_HEADS = 8
HA_DK = 128
HA_DV = 128
CHUNK = 64
LRU_WIDTH = 1024
LRU_HEADS = 4
LRU_BLOCK = LRU_WIDTH // LRU_HEADS
CONV_WIDTH = 4
RG_C = 8.0
N_EXPERTS = 32
TOP_K = 4
D_FF = 1024
SWIGLU_LIMIT = 7.0
SWIGLU_ALPHA = 1.702
EPS = 1e-5

SUBLANES = 8
LANES = 128
MXU_COLS = 256
TSH = 256
TSL = 512
TM = 512
TF = 512
MOE_PARTS = (1, 3)
FINAL_PARTS = (1, 1, 2)
VMEM_LIMIT = 56 * 1024 * 1024
SC_WINDOW = 128
SC_ROW = 128

NT_DIMS = (((1,), (1,)), ((), ()))
TN_DIMS = (((0,), (0,)), ((), ()))


def _rms(x, w):
    return x * lax.rsqrt(jnp.mean(x * x, axis=-1, keepdims=True) + EPS) * w


def _split3(v):
    hi = v.astype(BF16)
    r1 = v - hi.astype(F32)
    mid = r1.astype(BF16)
    lo = (r1 - mid.astype(F32)).astype(BF16)
    return hi, mid, lo


def _hgrn_kernel(x_ref, nw_ref, wa_ref, wga_ref, lbraw_ref, gnw_ref, wpa_ref, tri_ref, out_ref,
                 st_ref, qin_ref, kin_ref, kdec_ref, v_ref, on_ref):
    n_chunks = TSH // CHUNK
    hk = HA_HEADS * HA_DK

    @pl.when(pl.program_id(1) == 0)
    def _():
        st_ref[...] = jnp.zeros_like(st_ref)

    hb = _rms(x_ref[0], nw_ref[...]).astype(BF16)

    lbr = lbraw_ref[...]
    rows = [lbr[j:j + 1, :] for j in range(lbr.shape[0])]
    mx = functools.reduce(jnp.maximum, rows)
    es = [jnp.exp(r - mx) for r in rows]
    lb = es[0] / functools.reduce(lambda a, b: a + b, es)

    fl = jnp.dot(hb, wa_ref[:, hk:2 * hk], preferred_element_type=F32)
    f = lb + (1.0 - lb) * jax.nn.sigmoid(fl)
    lf = jnp.log(f)
    k = 1.0 - f
    tri = tri_ref[...]
    bcum = functools.reduce(lambda a, b: a + b,
                            [jnp.dot(tri, p, preferred_element_type=F32) for p in _split3(lf)])
    b_last = [bcum[c * CHUNK + CHUNK - 1:c * CHUNK + CHUNK, :] for c in range(n_chunks)]
    bl_full = jnp.concatenate([jnp.broadcast_to(b, (CHUNK, hk)) for b in b_last], axis=0)

    q = jnp.dot(hb, wa_ref[:, 0:hk], preferred_element_type=F32)
    qin_ref[...] = (jax.nn.silu(q) * jnp.exp(bcum)).astype(BF16)
    kin_ref[...] = (k * jnp.exp(-bcum)).astype(BF16)
    kdec_ref[...] = (k * jnp.exp(bl_full - bcum)).astype(BF16)
    v_ref[...] = jnp.dot(hb, wa_ref[:, 2 * hk:3 * hk], preferred_element_type=F32).astype(BF16)

    merge_gate = jax.nn.sigmoid(jnp.dot(hb, wga_ref[...], preferred_element_type=F32))

    causal = (lax.broadcasted_iota(jnp.int32, (CHUNK, CHUNK), 0)
              >= lax.broadcasted_iota(jnp.int32, (CHUNK, CHUNK), 1))
    sts = [st_ref[h] for h in range(HA_HEADS)]
    out_gate = None
    for c in range(n_chunks):
        if c == 1:
            g = jnp.dot(hb, wa_ref[:, 3 * hk:4 * hk], preferred_element_type=F32)
            out_gate = jax.nn.silu(g) * gnw_ref[...]
        r = slice(c * CHUNK, (c + 1) * CHUNK)
        edec = jnp.exp(b_last[c])
        for h in range(HA_HEADS):
            l = slice(h * HA_DK, (h + 1) * HA_DK)
            qh, kh, kd, vh = qin_ref[r, l], kin_ref[r, l], kdec_ref[r, l], v_ref[r, l]
            s = lax.dot_general(qh, kh, NT_DIMS, preferred_element_type=F32)
            s = jnp.where(causal, s, 0.0).astype(BF16)
            o = (jnp.dot(s, vh, preferred_element_type=F32)
                 + jnp.dot(qh, sts[h].T.astype(BF16), preferred_element_type=F32))
            sts[h] = sts[h] * edec[:, l] + lax.dot_general(vh, kd, TN_DIMS, preferred_element_type=F32)
            ms = jnp.mean(o * o, axis=-1, keepdims=True)
            on_ref[r, l] = o * lax.rsqrt(ms + EPS)
    for h in range(HA_HEADS):
        st_ref[h] = sts[h]

    ya = jnp.dot((on_ref[...] * out_gate).astype(BF16), wpa_ref[...], preferred_element_type=F32)
    out_ref[0] = merge_gate * ya


def _hgrn_call(x, nw, wa, wga, lb_raw, gnw, wpa, tri):
    b, s, d = x.shape
    hk = HA_HEADS * HA_DK
    const = lambda shape: pl.BlockSpec(shape, lambda i, j: (0,) * len(shape))
    return pl.pallas_call(
        _hgrn_kernel,
        grid=(b, s // TSH),
        in_specs=[
            pl.BlockSpec((1, TSH, d), lambda i, j: (i, j, 0)),
            const((1, d)), const((d, 4 * hk)), const((d, d)), const(lb_raw.shape), const((1, hk)),
            const((hk, d)), const((TSH, TSH)),
        ],
        out_specs=pl.BlockSpec((1, TSH, d), lambda i, j: (i, j, 0)),
        out_shape=jax.ShapeDtypeStruct((b, s, d), F32),
        scratch_shapes=[
            pltpu.VMEM((HA_HEADS, HA_DV, HA_DK), F32),
            pltpu.VMEM((TSH, hk), BF16), pltpu.VMEM((TSH, hk), BF16), pltpu.VMEM((TSH, hk), BF16),
            pltpu.VMEM((TSH, hk), BF16), pltpu.VMEM((TSH, hk), F32),
        ],
        compiler_params=pltpu.CompilerParams(
            dimension_semantics=("arbitrary", "arbitrary"), vmem_limit_bytes=VMEM_LIMIT),
        name="hgrn",
    )(x, nw, wa, wga, lb_raw, gnw, wpa, tri)


def _lru_kernel(x_ref, ya_ref, nw_ref, wb_ref, cw_ref, cb_ref, wra_ref, bra_ref, wrx_ref, brx_ref, lam_ref,
                wpb_ref, wout_ref, nffn_ref, wrt_ref, brt_ref, ustr_ref,
                x1_ref, h2_ref, ri_ref, rw_ref, cnt_ref,
                xbuf_ref, a_ref, u_ref, hl_ref, pp_ref, hc_ref, carry_ref):
    seg = TSL // SUBLANES
    pitch = seg + SUBLANES
    w = LRU_WIDTH
    bi, si = pl.program_id(0), pl.program_id(1)

    @pl.when(si == 0)
    def _():
        xbuf_ref[0:SUBLANES, :] = jnp.zeros((SUBLANES, w), F32)
        hc_ref[...] = jnp.zeros_like(hc_ref)

    @pl.when((bi == 0) & (si == 0))
    def _():
        carry_ref[...] = jnp.zeros_like(carry_ref)

    x = x_ref[0]
    hb = _rms(x, nw_ref[...]).astype(BF16)

    xb = jnp.dot(hb, wb_ref[:, 0:w], preferred_element_type=F32)
    xbuf_ref[SUBLANES:SUBLANES + TSL, :] = xb
    cw = cw_ref[...]
    xc = cb_ref[...] + cw[CONV_WIDTH - 1:CONV_WIDTH, :] * xb
    for kk in range(CONV_WIDTH - 1):
        shift = CONV_WIDTH - 1 - kk
        xc = xc + cw[kk:kk + 1, :] * xbuf_ref[SUBLANES - shift:SUBLANES - shift + TSL, :]
    xbuf_ref[0:SUBLANES, :] = xbuf_ref[TSL:TSL + SUBLANES, :]

    xcb = xc.astype(BF16)
    r_parts, i_parts = [], []
    for j in range(LRU_HEADS):
        l = slice(j * LRU_BLOCK, (j + 1) * LRU_BLOCK)
        r_parts.append(jnp.dot(xcb[:, l], wra_ref[j], preferred_element_type=F32))
        i_parts.append(jnp.dot(xcb[:, l], wrx_ref[j], preferred_element_type=F32))
    rg = jax.nn.sigmoid(jnp.concatenate(r_parts, axis=1) + bra_ref[...])
    ig = jax.nn.sigmoid(jnp.concatenate(i_parts, axis=1) + brx_ref[...])
    log_a = RG_C * rg * jax.nn.log_sigmoid(lam_ref[...])
    a = jnp.exp(log_a)
    m2 = 1.0 - a * a
    mult = jnp.where(m2 > 0.0, m2 * lax.rsqrt(m2), 0.0)
    is_first = (lax.broadcasted_iota(jnp.int32, (TSL, 1), 0) == 0) & (si == 0)
    mult = jnp.where(is_first, 1.0, mult)
    u = xc * ig * mult
    gelu_gate = jax.nn.gelu(jnp.dot(hb, wb_ref[:, w:2 * w], preferred_element_type=F32))
    merge_gate = jax.nn.sigmoid(jnp.dot(hb, wb_ref[:, 2 * w:3 * w], preferred_element_type=F32))

    n_groups = w // LANES
    group_rows = SUBLANES * pitch
    for gi in range(n_groups):
        for j in range(SUBLANES):
            dst = slice(gi * group_rows + j * pitch, gi * group_rows + j * pitch + seg)
            a_ref[dst, :] = a[j * seg:(j + 1) * seg, gi * LANES:(gi + 1) * LANES]
            u_ref[dst, :] = u[j * seg:(j + 1) * seg, gi * LANES:(gi + 1) * LANES]

    def step(i, hp):
        hs, ps = hp
        new_h, new_p = [], []
        for gi in range(n_groups):
            rows = pl.ds(gi * group_rows + i, SUBLANES, stride=pitch)
            ai = a_ref[rows, :]
            h = ai * hs[gi] + u_ref[rows, :]
            p = ai * ps[gi]
            hl_ref[rows, :] = h
            pp_ref[rows, :] = p
            new_h.append(h)
            new_p.append(p)
        return tuple(new_h), tuple(new_p)

    init = (tuple(jnp.zeros((SUBLANES, LANES), F32) for _ in range(n_groups)),
            tuple(jnp.ones((SUBLANES, LANES), F32) for _ in range(n_groups)))
    h_end, p_end = lax.fori_loop(0, seg, step, init, unroll=True)
    h_end = jnp.concatenate(h_end, axis=1)
    p_end = jnp.concatenate(p_end, axis=1)
    c = hc_ref[...]
    h_parts = []
    for j in range(SUBLANES):
        src = [slice(gi * group_rows + j * pitch, gi * group_rows + j * pitch + seg) for gi in range(n_groups)]
        hl = jnp.concatenate([hl_ref[r, :] for r in src], axis=1)
        pp = jnp.concatenate([pp_ref[r, :] for r in src], axis=1)
        h_parts.append(hl + pp * c)
        c = h_end[j:j + 1, :] + p_end[j:j + 1, :] * c
    hc_ref[...] = c
    h_full = jnp.concatenate(h_parts, axis=0)

    ob = (h_full * gelu_gate).astype(BF16)
    yb = jnp.dot(ob, wpb_ref[...], preferred_element_type=F32)
    merged = ya_ref[0] + merge_gate * yb
    x1 = x + jnp.dot(merged.astype(BF16), wout_ref[...], preferred_element_type=F32)
    x1_ref[0] = x1

    h2 = _rms(x1, nffn_ref[...])
    h_hi = h2.astype(BF16)
    h2_ref[0] = h_hi
    h_lo = (h2 - h_hi.astype(F32)).astype(BF16)
    wt = wrt_ref[...]
    w_hi = wt.astype(BF16)
    w_lo = (wt - w_hi.astype(F32)).astype(BF16)
    both = lax.dot_general(jnp.concatenate([w_hi, w_lo], axis=0), h_hi, NT_DIMS, preferred_element_type=F32)
    logits = (both[:N_EXPERTS] + both[N_EXPERTS:]
              + lax.dot_general(w_hi, h_lo, NT_DIMS, preferred_element_type=F32)) + brt_ref[...]

    eio = lax.broadcasted_iota(jnp.int32, (N_EXPERTS, TSL), 0)
    idxs, vals = [], []
    for _ in range(TOP_K):
        m = jnp.max(logits, axis=0, keepdims=True)
        ik = jnp.min(jnp.where(logits == m, eio, N_EXPERTS), axis=0, keepdims=True)
        idxs.append(ik)
        vals.append(m)
        logits = jnp.where(eio == ik, -jnp.inf, logits)
    exps = [jnp.exp(v - vals[0]) for v in vals]
    den = functools.reduce(lambda p, q: p + q, exps)
    onehots = [(eio == ik) for ik in idxs]
    oh = functools.reduce(lambda p, q: p + q, [o.astype(F32) for o in onehots])
    cum = jnp.dot(oh.astype(BF16), ustr_ref[...], preferred_element_type=F32) + carry_ref[:, 0:1]
    poss = [jnp.sum(jnp.where(o, cum, 0.0), axis=0, keepdims=True).astype(jnp.int32) for o in onehots]
    carry_ref[...] = carry_ref[...] + jnp.sum(oh, axis=1, keepdims=True)
    ri_ref[...] = jnp.concatenate(idxs + poss, axis=0)
    rw_ref[...] = jnp.concatenate([e / den for e in exps] + [jnp.zeros((SUBLANES - TOP_K, TSL), F32)], axis=0)
    cnt_ref[...] = carry_ref[...]


def _lru_call(x, ya, nw, wb, cw, cb, wra, bra, wrx, brx, lam, wpb, wout, nffn, wrt, brt, ustr):
    b, s, d = x.shape
    w = LRU_WIDTH
    scan_shape = ((TSL + SUBLANES * SUBLANES) * (w // LANES), LANES)
    const = lambda shape: pl.BlockSpec(shape, lambda i, j: (0,) * len(shape))
    tok = pl.BlockSpec((1, TSL, d), lambda i, j: (i, j, 0))
    rt = pl.BlockSpec((SUBLANES, TSL), lambda i, j: (0, i * (s // TSL) + j))
    return pl.pallas_call(
        _lru_kernel,
        grid=(b, s // TSL),
        in_specs=[
            tok, tok, const((1, d)), const((d, 3 * w)), const((CONV_WIDTH, w)), const((1, w)),
            const((LRU_HEADS, LRU_BLOCK, LRU_BLOCK)), const((1, w)),
            const((LRU_HEADS, LRU_BLOCK, LRU_BLOCK)), const((1, w)), const((1, w)),
            const((w, d)), const((d, d)), const((1, d)), const((N_EXPERTS, d)), const((N_EXPERTS, 1)),
            const((TSL, TSL)),
        ],
        out_specs=[tok, tok, rt, rt, const((N_EXPERTS, LANES))],
        out_shape=[
            jax.ShapeDtypeStruct((b, s, d), F32),
            jax.ShapeDtypeStruct((b, s, d), BF16),
            jax.ShapeDtypeStruct((SUBLANES, b * s), jnp.int32),
            jax.ShapeDtypeStruct((SUBLANES, b * s), F32),
            jax.ShapeDtypeStruct((N_EXPERTS, LANES), F32),
        ],
        scratch_shapes=[
            pltpu.VMEM((TSL + SUBLANES, w), F32),
            pltpu.VMEM(scan_shape, F32), pltpu.VMEM(scan_shape, F32),
            pltpu.VMEM(scan_shape, F32), pltpu.VMEM(scan_shape, F32),
            pltpu.VMEM((1, w), F32), pltpu.VMEM((N_EXPERTS, LANES), F32),
        ],
        compiler_params=pltpu.CompilerParams(
            dimension_semantics=("arbitrary", "arbitrary"), vmem_limit_bytes=VMEM_LIMIT),
        name="lru_router",
    )(x, ya, nw, wb, cw, cb, wra, bra, wrx, brx, lam, wpb, wout, nffn, wrt, brt, ustr)


def _invert_slots(slot_flat, n_rows, t):
    n = slot_flat.shape[0]
    vals = jnp.broadcast_to((jnp.arange(n, dtype=jnp.int32) % t)[:, None], (n, SC_ROW))
    mesh = plsc.VectorSubcoreMesh(core_axis_name="core", subcore_axis_name="subcore")

    @functools.partial(pl.kernel, out_type=jax.ShapeDtypeStruct((n_rows, SC_ROW), jnp.int32), mesh=mesh,
                       scratch_types=[])
    def scatter_kernel(x_hbm, i_hbm, o_hbm):
        def body(x_vmem, i_vmem):
            pltpu.sync_copy(x_vmem, o_hbm.at[i_vmem.at[0]])

        pltpu.emit_pipeline(
            body,
            grid=(n // SC_WINDOW,),
            in_specs=[pl.BlockSpec((SC_WINDOW, SC_ROW), lambda i: (i, 0)),
                      pl.BlockSpec((1, SC_WINDOW), lambda i: (0, i))],
            out_specs=[],
            core_axis_name=("core", "subcore"),
            dimension_semantics=(pltpu.PARALLEL,),
        )(x_hbm, i_hbm)

    return scatter_kernel(vals, slot_flat.reshape(1, n))[:, 0]


def _moe_kernel(te_ref, nu_ref, xs_ref, wgu_ref, bg_ref, bl_ref, wd_ref, bd_ref, perm_ref, *rest):
    y_ref, wg_s, wl_s, wd_s = rest[-4:]
    i = pl.program_id(0)
    half = MXU_COLS // 2
    new_expert = (i == 0) | (te_ref[i] != te_ref[jnp.maximum(i - 1, 0)])

    @pl.when(new_expert & (i < nu_ref[0]))
    def _():
        for blk in range(2 * D_FF // MXU_COLS):
            wblk = wgu_ref[0, :, blk * MXU_COLS:(blk + 1) * MXU_COLS].astype(BF16)
            pw = jnp.dot(wblk, perm_ref[...], preferred_element_type=F32)
            wg_s[:, blk * half:(blk + 1) * half] = pw[:, :half].astype(BF16)
            wl_s[:, blk * half:(blk + 1) * half] = pw[:, half:].astype(BF16)
        wd_s[...] = wd_ref[0].astype(BF16)

    @pl.when(i < nu_ref[0])
    def _():
        xs = xs_ref[...]
        g = jnp.dot(xs, wg_s[...], preferred_element_type=F32) + bg_ref[0]
        l = jnp.dot(xs, wl_s[...], preferred_element_type=F32) + bl_ref[0]
        g = jnp.minimum(g, SWIGLU_LIMIT)
        l = jnp.clip(l, -SWIGLU_LIMIT, SWIGLU_LIMIT)
        act = g * jax.nn.sigmoid(SWIGLU_ALPHA * g) * (l + 1.0)
        y = jnp.dot(act.astype(BF16), wd_s[...], preferred_element_type=F32) + bd_ref[0]
        y_ref[...] = y.astype(y_ref.dtype)

    @pl.when(i >= nu_ref[0])
    def _():
        y_ref[...] = jnp.zeros_like(y_ref)


def _moe_call(tile_expert, n_used, xs, wgu, bg, bl, wd, bd, *, y_prev=None, base_tile=0, total_rows=None):
    p, d = xs.shape
    n_tiles = p // TM
    total_rows = p if total_rows is None else total_rows
    tile_expert = lax.slice(tile_expert, (base_tile,), (base_tile + n_tiles,))
    n_used = jnp.clip(n_used - base_tile, 1, n_tiles)
    half = MXU_COLS // 2
    r = lax.broadcasted_iota(jnp.int32, (MXU_COLS, MXU_COLS), 0)
    c = lax.broadcasted_iota(jnp.int32, (MXU_COLS, MXU_COLS), 1)
    perm = (r == jnp.where(c < half, 2 * c, 2 * (c - half) + 1)).astype(BF16)
    row = lambda i, te, nu: (jnp.minimum(i, nu[0] - 1), 0)
    ex3 = lambda i, te, nu: (te[i], 0, 0)
    in_specs = [
        pl.BlockSpec((TM, d), row),
        pl.BlockSpec((1, d, 2 * D_FF), ex3),
        pl.BlockSpec((1, 1, D_FF), ex3), pl.BlockSpec((1, 1, D_FF), ex3),
        pl.BlockSpec((1, D_FF, d), ex3), pl.BlockSpec((1, 1, d), ex3),
        pl.BlockSpec((MXU_COLS, MXU_COLS), lambda i, te, nu: (0, 0)),
    ]
    operands = [tile_expert, n_used, xs, wgu, bg, bl, wd, bd, perm]
    aliases = {}
    if y_prev is not None:
        in_specs.append(pl.BlockSpec(memory_space=pl.ANY))
        aliases = {len(operands): 0}
        operands.append(y_prev)
    grid_spec = pltpu.PrefetchScalarGridSpec(
        num_scalar_prefetch=2,
        grid=(n_tiles,),
        in_specs=in_specs,
        out_specs=pl.BlockSpec((TM, d), lambda i, te, nu: (i + base_tile, 0)),
        scratch_shapes=[pltpu.VMEM((d, D_FF), BF16), pltpu.VMEM((d, D_FF), BF16), pltpu.VMEM((D_FF, d), BF16)],
    )
    return pl.pallas_call(
        _moe_kernel,
        grid_spec=grid_spec,
        out_shape=jax.ShapeDtypeStruct((total_rows, d), BF16),
        input_output_aliases=aliases,
        compiler_params=pltpu.CompilerParams(
            dimension_semantics=("arbitrary",), vmem_limit_bytes=VMEM_LIMIT),
        name="moe_experts",
    )(*operands)


def _final_kernel(x1_ref, yg_ref, gw_ref, nw_ref, *rest):
    out_ref = rest[-1]
    gw = gw_ref[...]
    x2 = x1_ref[...]
    for k in range(TOP_K):
        x2 = x2 + gw[:, k:k + 1] * yg_ref[k].astype(F32)
    out_ref[...] = _rms(x2, nw_ref[...])


def _final_call(x1, yg, gw, nw, *, out_prev=None, base_block=0):
    t, d = x1.shape
    blk = pl.BlockSpec((TF, d), lambda i: (i + base_block, 0))
    in_specs = [blk, pl.BlockSpec((TOP_K, TF, d), lambda i: (0, i, 0)),
                pl.BlockSpec((TF, TOP_K), lambda i: (i + base_block, 0)), pl.BlockSpec((1, d), lambda i: (0, 0))]
    operands = [x1, yg, gw, nw]
    aliases = {}
    if out_prev is not None:
        in_specs.append(pl.BlockSpec(memory_space=pl.ANY))
        aliases = {len(operands): 0}
        operands.append(out_prev)
    return pl.pallas_call(
        _final_kernel,
        grid=(yg.shape[1] // TF,),
        in_specs=in_specs,
        out_specs=blk,
        out_shape=jax.ShapeDtypeStruct((t, d), F32),
        input_output_aliases=aliases,
        compiler_params=pltpu.CompilerParams(dimension_semantics=("arbitrary",)),
        name="final_norm",
    )(*operands)


def _chunk_masks():
    r = lax.broadcasted_iota(jnp.int32, (TSH, TSH), 0)
    c = lax.broadcasted_iota(jnp.int32, (TSH, TSH), 1)
    tri = ((r // CHUNK == c // CHUNK) & (r >= c)).astype(BF16)
    r = lax.broadcasted_iota(jnp.int32, (TSL, TSL), 0)
    c = lax.broadcasted_iota(jnp.int32, (TSL, TSL), 1)
    ustr = (r < c).astype(BF16)
    return tri, ustr


def kernel(x, norm_mix, w_in, lb_raw, gn_w, conv_w, conv_b, w_rg_a, b_rg_a, w_rg_x, b_rg_x, lru_lambda,
           w_proj_a, w_proj_b, w_out, norm_ffn, w_router, b_router, w_gu, b_gu, w_down, b_down, norm_final):
    b, s, d = x.shape
    t = b * s
    assert w_in.shape[0] == 1 and s % TSH == 0 and s % TSL == 0
    hk = HA_HEADS * HA_DK
    w = LRU_WIDTH
    tri, ustr = _chunk_masks()

    wi = w_in[0]
    o_lru = 4 * hk
    wa = wi[:, 0:o_lru].astype(BF16)
    wb = jnp.concatenate([wi[:, o_lru:o_lru + 2 * w], wi[:, o_lru + 2 * w + d:o_lru + 2 * w + 2 * d]],
                         axis=1).astype(BF16)
    wga = wi[:, o_lru + 2 * w:o_lru + 2 * w + d].astype(BF16)
    row = lambda v: v.reshape(1, -1)

    ya = _hgrn_call(x, row(norm_mix[0]), wa, wga, lb_raw, row(gn_w[0]), w_proj_a[0].astype(BF16), tri)
    x1, h2, ri, rw, cnt = _lru_call(
        x, ya, row(norm_mix[0]), wb, conv_w[0], row(conv_b[0]), w_rg_a[0].astype(BF16), row(b_rg_a[0]),
        w_rg_x[0].astype(BF16), row(b_rg_x[0]), row(lru_lambda[0]), w_proj_b[0].astype(BF16),
        w_out[0].astype(BF16), row(norm_ffn[0]), w_router[0].T, b_router[0].reshape(-1, 1), ustr)

    idx, pos, gate_w = ri[0:TOP_K], ri[TOP_K:2 * TOP_K], rw[0:TOP_K]
    counts = cnt[:, 0].astype(jnp.int32)
    padded = ((counts + TM - 1) // TM) * TM
    ends = jnp.cumsum(padded)
    offs = ends - padded
    idx_d, slot = idx.reshape(-1, LANES), pos.reshape(-1, LANES)
    for e in range(N_EXPERTS):
        slot = slot + jnp.where(idx_d == e, offs[e], 0)
    slot = slot.reshape(TOP_K, t)
    n_rows = TOP_K * t + N_EXPERTS * TM
    n_tiles = n_rows // TM
    tile_expert = jnp.minimum(
        jnp.sum(jnp.arange(n_tiles, dtype=jnp.int32)[:, None] * TM >= ends[None, :], axis=1),
        N_EXPERTS - 1).astype(jnp.int32)
    n_used = (ends[-1:] // TM).astype(jnp.int32)

    n_pad = padded - counts
    pads_before = jnp.cumsum(n_pad) - n_pad
    j = jnp.arange(TM, dtype=jnp.int32)[None, :]
    e_base = (jnp.arange(N_EXPERTS, dtype=jnp.int32) * TM)[:, None]
    pad_slot = jnp.where(j < n_pad[:, None], (offs + counts)[:, None] + j,
                         ends[-1] + e_base + j - (pads_before + n_pad)[:, None])
    all_slots = jnp.concatenate([slot.reshape(-1), pad_slot.reshape(-1)])
    tok_of_slot = _invert_slots(all_slots, n_rows, t)
    def bounds(total, unit, parts):
        cuts = [total * sum(parts[:k]) // sum(parts) for k in range(len(parts) + 1)]
        assert all(c % unit == 0 for c in cuts)
        return list(zip(cuts[:-1], cuts[1:]))

    h2r, x1r = h2.reshape(t, d), x1.reshape(t, d)
    bgs, bls = b_gu[0][:, 0::2].reshape(N_EXPERTS, 1, D_FF), b_gu[0][:, 1::2].reshape(N_EXPERTS, 1, D_FF)
    bds = b_down[0].reshape(N_EXPERTS, 1, d)
    row_parts = bounds(n_rows, TM, MOE_PARTS)
    assert row_parts[-1][0] < TOP_K * t
    y = None
    for lo, hi in row_parts:
        xs = h2r.at[tok_of_slot[lo:hi]].get(mode="promise_in_bounds")
        y = _moe_call(tile_expert, n_used, xs, w_gu[0], bgs, bls, w_down[0], bds,
                      y_prev=y, base_tile=lo // TM, total_rows=n_rows)

    gw = gate_w.T
    out = None
    for lo, hi in bounds(t, TF, FINAL_PARTS):
        yg = y.at[slot[:, lo:hi]].get(mode="promise_in_bounds")
        out = _final_call(x1r, yg, gw, row(norm_final), out_prev=out, base_block=lo // TF)
    return out.reshape(b, s, d)
```

```python
import functools

import jax
import jax.numpy as jnp
from jax import lax
from jax.experimental import pallas as pl
from jax.experimental.pallas import tpu as pltpu
from jax.experimental.pallas import tpu_sc as plsc

F32 = jnp.float32
BF16 = jnp.bfloat16

D_MODEL = 1024
HA_HEADS = 8
HA_DK = 128
HA_DV = 128
CHUNK = 64
LRU_WIDTH = 1024
LRU_HEADS = 4
LRU_BLOCK = LRU_WIDTH // LRU_HEADS
CONV_WIDTH = 4
RG_C = 8.0
N_EXPERTS = 32
TOP_K = 4
D_FF = 1024
SWIGLU_LIMIT = 7.0
SWIGLU_ALPHA = 1.702
EPS = 1e-5

SUBLANES = 8
LANES = 128
MXU_COLS = 256
TSH = 256
TSL = 512
TM = 512
TF = 512
MOE_PARTS = (1, 3)
FINAL_PARTS = (1, 1, 2)
VMEM_LIMIT = 56 * 1024 * 1024
SC_WINDOW = 128
SC_ROW = 128

NT_DIMS = (((1,), (1,)), ((), ()))
TN_DIMS = (((0,), (0,)), ((), ()))


def _rms(x, w):
    return x * lax.rsqrt(jnp.mean(x * x, axis=-1, keepdims=True) + EPS) * w


def _split3(v):
    hi = v.astype(BF16)
    r1 = v - hi.astype(F32)
    mid = r1.astype(BF16)
    lo = (r1 - mid.astype(F32)).astype(BF16)
    return hi, mid, lo


def _hgrn_kernel(x_ref, nw_ref, wa_ref, wga_ref, lbraw_ref, gnw_ref, wpa_ref, tri_ref, out_ref,
                 st_ref, qin_ref, kin_ref, kdec_ref, v_ref, on_ref):
    n_chunks = TSH // CHUNK
    hk = HA_HEADS * HA_DK

    @pl.when(pl.program_id(1) == 0)
    def _():
        st_ref[...] = jnp.zeros_like(st_ref)

    hb = _rms(x_ref[0], nw_ref[...]).astype(BF16)

    lbr = lbraw_ref[...]
    rows = [lbr[j:j + 1, :] for j in range(lbr.shape[0])]
    mx = functools.reduce(jnp.maximum, rows)
    es = [jnp.exp(r - mx) for r in rows]
    lb = es[0] / functools.reduce(lambda a, b: a + b, es)

    fl = jnp.dot(hb, wa_ref[:, hk:2 * hk], preferred_element_type=F32)
    qs = jax.nn.silu(jnp.dot(hb, wa_ref[:, 0:hk], preferred_element_type=F32))
    v_ref[...] = jnp.dot(hb, wa_ref[:, 2 * hk:3 * hk], preferred_element_type=F32).astype(BF16)
    f = lb + (1.0 - lb) * jax.nn.sigmoid(fl)
    lf = jnp.log(f)
    k = 1.0 - f
    tri = tri_ref[...]
    bcum = functools.reduce(lambda a, b: a + b,
                            [jnp.dot(tri, p, preferred_element_type=F32) for p in _split3(lf)])
    b_last = [bcum[c * CHUNK + CHUNK - 1:c * CHUNK + CHUNK, :] for c in range(n_chunks)]
    bl_full = jnp.concatenate([jnp.broadcast_to(b, (CHUNK, hk)) for b in b_last], axis=0)

    merge_gate = jax.nn.sigmoid(jnp.dot(hb, wga_ref[...], preferred_element_type=F32))
    g = jnp.dot(hb, wa_ref[:, 3 * hk:4 * hk], preferred_element_type=F32)
    out_gate = jax.nn.silu(g) * gnw_ref[...]
    qin_ref[...] = (qs * jnp.exp(bcum)).astype(BF16)
    kin_ref[...] = (k * jnp.exp(-bcum)).astype(BF16)
    kdec_ref[...] = (k * jnp.exp(bl_full - bcum)).astype(BF16)

    causal = (lax.broadcasted_iota(jnp.int32, (CHUNK, CHUNK), 0)
              >= lax.broadcasted_iota(jnp.int32, (CHUNK, CHUNK), 1))
    sts = [st_ref[h] for h in range(HA_HEADS)]
    for c in range(n_chunks):
        r = slice(c * CHUNK, (c + 1) * CHUNK)
        edec = jnp.exp(b_last[c])
        for h in range(HA_HEADS):
            l = slice(h * HA_DK, (h + 1) * HA_DK)
            qh, kh, kd, vh = qin_ref[r, l], kin_ref[r, l], kdec_ref[r, l], v_ref[r, l]
            s = lax.dot_general(qh, kh, NT_DIMS, preferred_element_type=F32)
            s = jnp.where(causal, s, 0.0).astype(BF16)
            o = (jnp.dot(s, vh, preferred_element_type=F32)
                 + jnp.dot(qh, sts[h].T.astype(BF16), preferred_element_type=F32))
            sts[h] = sts[h] * edec[:, l] + lax.dot_general(vh, kd, TN_DIMS, preferred_element_type=F32)
            ms = jnp.mean(o * o, axis=-1, keepdims=True)
            on_ref[r, l] = o * lax.rsqrt(ms + EPS)
    for h in range(HA_HEADS):
        st_ref[h] = sts[h]

    ya = jnp.dot((on_ref[...] * out_gate).astype(BF16), wpa_ref[...], preferred_element_type=F32)
    out_ref[0] = merge_gate * ya


def _hgrn_call(x, nw, wa, wga, lb_raw, gnw, wpa, tri):
    b, s, d = x.shape
    hk = HA_HEADS * HA_DK
    const = lambda shape: pl.BlockSpec(shape, lambda i, j: (0,) * len(shape))
    return pl.pallas_call(
        _hgrn_kernel,
        grid=(b, s // TSH),
        in_specs=[
            pl.BlockSpec((1, TSH, d), lambda i, j: (i, j, 0)),
            const((1, d)), const((d, 4 * hk)), const((d, d)), const(lb_raw.shape), const((1, hk)),
            const((hk, d)), const((TSH, TSH)),
        ],
        out_specs=pl.BlockSpec((1, TSH, d), lambda i, j: (i, j, 0)),
        out_shape=jax.ShapeDtypeStruct((b, s, d), F32),
        scratch_shapes=[
            pltpu.VMEM((HA_HEADS, HA_DV, HA_DK), F32),
            pltpu.VMEM((TSH, hk), BF16), pltpu.VMEM((TSH, hk), BF16), pltpu.VMEM((TSH, hk), BF16),
            pltpu.VMEM((TSH, hk), BF16), pltpu.VMEM((TSH, hk), F32),
        ],
        compiler_params=pltpu.CompilerParams(
            dimension_semantics=("arbitrary", "arbitrary"), vmem_limit_bytes=VMEM_LIMIT),
        name="hgrn",
    )(x, nw, wa, wga, lb_raw, gnw, wpa, tri)


def _lru_kernel(x_ref, ya_ref, nw_ref, wb_ref, cw_ref, cb_ref, wra_ref, bra_ref, wrx_ref, brx_ref, lam_ref,
                wpb_ref, wout_ref, nffn_ref, wrt_ref, brt_ref, ustr_ref,
                x1_ref, h2_ref, ri_ref, rw_ref, cnt_ref,
                xbuf_ref, a_ref, u_ref, hl_ref, pp_ref, hc_ref, carry_ref):
    seg = TSL // SUBLANES
    pitch = seg + SUBLANES
    w = LRU_WIDTH
    bi, si = pl.program_id(0), pl.program_id(1)

    @pl.when(si == 0)
    def _():
        xbuf_ref[0:SUBLANES, :] = jnp.zeros((SUBLANES, w), F32)
        hc_ref[...] = jnp.zeros_like(hc_ref)

    @pl.when((bi == 0) & (si == 0))
    def _():
        carry_ref[...] = jnp.zeros_like(carry_ref)

    x = x_ref[0]
    hb = _rms(x, nw_ref[...]).astype(BF16)

    xb = jnp.dot(hb, wb_ref[:, 0:w], preferred_element_type=F32)
    xbuf_ref[SUBLANES:SUBLANES + TSL, :] = xb
    cw = cw_ref[...]
    xc = cb_ref[...] + cw[CONV_WIDTH - 1:CONV_WIDTH, :] * xb
    for kk in range(CONV_WIDTH - 1):
        shift = CONV_WIDTH - 1 - kk
        xc = xc + cw[kk:kk + 1, :] * xbuf_ref[SUBLANES - shift:SUBLANES - shift + TSL, :]
    xbuf_ref[0:SUBLANES, :] = xbuf_ref[TSL:TSL + SUBLANES, :]

    xcb = xc.astype(BF16)
    r_parts, i_parts = [], []
    for j in range(LRU_HEADS):
        l = slice(j * LRU_BLOCK, (j + 1) * LRU_BLOCK)
        r_parts.append(jnp.dot(xcb[:, l], wra_ref[j], preferred_element_type=F32))
        i_parts.append(jnp.dot(xcb[:, l], wrx_ref[j], preferred_element_type=F32))
    rg = jax.nn.sigmoid(jnp.concatenate(r_parts, axis=1) + bra_ref[...])
    ig = jax.nn.sigmoid(jnp.concatenate(i_parts, axis=1) + brx_ref[...])
    log_a = RG_C * rg * jax.nn.log_sigmoid(lam_ref[...])
    a = jnp.exp(log_a)
    m2 = 1.0 - a * a
    mult = jnp.where(m2 > 0.0, m2 * lax.rsqrt(m2), 0.0)
    is_first = (lax.broadcasted_iota(jnp.int32, (TSL, 1), 0) == 0) & (si == 0)
    mult = jnp.where(is_first, 1.0, mult)
    u = xc * ig * mult
    gelu_gate = jax.nn.gelu(jnp.dot(hb, wb_ref[:, w:2 * w], preferred_element_type=F32))
    merge_gate = jax.nn.sigmoid(jnp.dot(hb, wb_ref[:, 2 * w:3 * w], preferred_element_type=F32))

    n_groups = w // LANES
    group_rows = SUBLANES * pitch
    for gi in range(n_groups):
        for j in range(SUBLANES):
            dst = slice(gi * group_rows + j * pitch, gi * group_rows + j * pitch + seg)
            a_ref[dst, :] = a[j * seg:(j + 1) * seg, gi * LANES:(gi + 1) * LANES]
            u_ref[dst, :] = u[j * seg:(j + 1) * seg, gi * LANES:(gi + 1) * LANES]

    def step(i, hp):
        hs, ps = hp
        new_h, new_p = [], []
        for gi in range(n_groups):
            rows = pl.ds(gi * group_rows + i, SUBLANES, stride=pitch)
            ai = a_ref[rows, :]
            h = ai * hs[gi] + u_ref[rows, :]
            p = ai * ps[gi]
            hl_ref[rows, :] = h
            pp_ref[rows, :] = p
            new_h.append(h)
            new_p.append(p)
        return tuple(new_h), tuple(new_p)

    init = (tuple(jnp.zeros((SUBLANES, LANES), F32) for _ in range(n_groups)),
            tuple(jnp.ones((SUBLANES, LANES), F32) for _ in range(n_groups)))
    h_end, p_end = lax.fori_loop(0, seg, step, init, unroll=True)
    h_end = jnp.concatenate(h_end, axis=1)
    p_end = jnp.concatenate(p_end, axis=1)
    c = hc_ref[...]
    h_parts = []
    for j in range(SUBLANES):
        src = [slice(gi * group_rows + j * pitch, gi * group_rows + j * pitch + seg) for gi in range(n_groups)]
        hl = jnp.concatenate([hl_ref[r, :] for r in src], axis=1)
        pp = jnp.concatenate([pp_ref[r, :] for r in src], axis=1)
        h_parts.append(hl + pp * c)
        c = h_end[j:j + 1, :] + p_end[j:j + 1, :] * c
    hc_ref[...] = c
    h_full = jnp.concatenate(h_parts, axis=0)

    ob = (h_full * gelu_gate).astype(BF16)
    yb = jnp.dot(ob, wpb_ref[...], preferred_element_type=F32)
    merged = ya_ref[0] + merge_gate * yb
    x1 = x + jnp.dot(merged.astype(BF16), wout_ref[...], preferred_element_type=F32)
    x1_ref[0] = x1

    h2 = _rms(x1, nffn_ref[...])
    h_hi = h2.astype(BF16)
    h2_ref[0] = h_hi
    h_lo = (h2 - h_hi.astype(F32)).astype(BF16)
    wt = wrt_ref[...]
    w_hi = wt.astype(BF16)
    w_lo = (wt - w_hi.astype(F32)).astype(BF16)
    both = lax.dot_general(jnp.concatenate([w_hi, w_lo], axis=0), h_hi, NT_DIMS, preferred_element_type=F32)
    logits = (both[:N_EXPERTS] + both[N_EXPERTS:]
              + lax.dot_general(w_hi, h_lo, NT_DIMS, preferred_element_type=F32)) + brt_ref[...]

    eio = lax.broadcasted_iota(jnp.int32, (N_EXPERTS, TSL), 0)
    idxs, vals = [], []
    for _ in range(TOP_K):
        m = jnp.max(logits, axis=0, keepdims=True)
        ik = jnp.min(jnp.where(logits == m, eio, N_EXPERTS), axis=0, keepdims=True)
        idxs.append(ik)
        vals.append(m)
        logits = jnp.where(eio == ik, -jnp.inf, logits)
    exps = [jnp.exp(v - vals[0]) for v in vals]
    den = functools.reduce(lambda p, q: p + q, exps)
    onehots = [(eio == ik) for ik in idxs]
    oh = functools.reduce(lambda p, q: p + q, [o.astype(F32) for o in onehots])
    cum = jnp.dot(oh.astype(BF16), ustr_ref[...], preferred_element_type=F32) + carry_ref[:, 0:1]
    poss = [jnp.sum(jnp.where(o, cum, 0.0), axis=0, keepdims=True).astype(jnp.int32) for o in onehots]
    carry_ref[...] = carry_ref[...] + jnp.sum(oh, axis=1, keepdims=True)
    ri_ref[...] = jnp.concatenate(idxs + poss, axis=0)
    rw_ref[...] = jnp.concatenate([e / den for e in exps] + [jnp.zeros((SUBLANES - TOP_K, TSL), F32)], axis=0)
    cnt_ref[...] = carry_ref[...]


def _lru_call(x, ya, nw, wb, cw, cb, wra, bra, wrx, brx, lam, wpb, wout, nffn, wrt, brt, ustr):
    b, s, d = x.shape
    w = LRU_WIDTH
    scan_shape = ((TSL + SUBLANES * SUBLANES) * (w // LANES), LANES)
    const = lambda shape: pl.BlockSpec(shape, lambda i, j: (0,) * len(shape))
    tok = pl.BlockSpec((1, TSL, d), lambda i, j: (i, j, 0))
    rt = pl.BlockSpec((SUBLANES, TSL), lambda i, j: (0, i * (s // TSL) + j))
    return pl.pallas_call(
        _lru_kernel,
        grid=(b, s // TSL),
        in_specs=[
            tok, tok, const((1, d)), const((d, 3 * w)), const((CONV_WIDTH, w)), const((1, w)),
            const((LRU_HEADS, LRU_BLOCK, LRU_BLOCK)), const((1, w)),
            const((LRU_HEADS, LRU_BLOCK, LRU_BLOCK)), const((1, w)), const((1, w)),
            const((w, d)), const((d, d)), const((1, d)), const((N_EXPERTS, d)), const((N_EXPERTS, 1)),
            const((TSL, TSL)),
        ],
        out_specs=[tok, tok, rt, rt, const((N_EXPERTS, LANES))],
        out_shape=[
            jax.ShapeDtypeStruct((b, s, d), F32),
            jax.ShapeDtypeStruct((b, s, d), BF16),
            jax.ShapeDtypeStruct((SUBLANES, b * s), jnp.int32),
            jax.ShapeDtypeStruct((SUBLANES, b * s), F32),
            jax.ShapeDtypeStruct((N_EXPERTS, LANES), F32),
        ],
        scratch_shapes=[
            pltpu.VMEM((TSL + SUBLANES, w), F32),
            pltpu.VMEM(scan_shape, F32), pltpu.VMEM(scan_shape, F32),
            pltpu.VMEM(scan_shape, F32), pltpu.VMEM(scan_shape, F32),
            pltpu.VMEM((1, w), F32), pltpu.VMEM((N_EXPERTS, LANES), F32),
        ],
        compiler_params=pltpu.CompilerParams(
            dimension_semantics=("arbitrary", "arbitrary"), vmem_limit_bytes=VMEM_LIMIT),
        name="lru_router",
    )(x, ya, nw, wb, cw, cb, wra, bra, wrx, brx, lam, wpb, wout, nffn, wrt, brt, ustr)


def _invert_slots(slot_flat, n_rows, t):
    n = slot_flat.shape[0]
    vals = jnp.broadcast_to((jnp.arange(n, dtype=jnp.int32) % t)[:, None], (n, SC_ROW))
    mesh = plsc.VectorSubcoreMesh(core_axis_name="core", subcore_axis_name="subcore")

    @functools.partial(pl.kernel, out_type=jax.ShapeDtypeStruct((n_rows, SC_ROW), jnp.int32), mesh=mesh,
                       scratch_types=[])
    def scatter_kernel(x_hbm, i_hbm, o_hbm):
        def body(x_vmem, i_vmem):
            pltpu.sync_copy(x_vmem, o_hbm.at[i_vmem.at[0]])

        pltpu.emit_pipeline(
            body,
            grid=(n // SC_WINDOW,),
            in_specs=[pl.BlockSpec((SC_WINDOW, SC_ROW), lambda i: (i, 0)),
                      pl.BlockSpec((1, SC_WINDOW), lambda i: (0, i))],
            out_specs=[],
            core_axis_name=("core", "subcore"),
            dimension_semantics=(pltpu.PARALLEL,),
        )(x_hbm, i_hbm)

    return scatter_kernel(vals, slot_flat.reshape(1, n))[:, 0]


def _moe_kernel(te_ref, nu_ref, xs_ref, wgu_ref, bg_ref, bl_ref, wd_ref, bd_ref, perm_ref, *rest):
    y_ref, wg_s, wl_s, wd_s = rest[-4:]
    i = pl.program_id(0)
    half = MXU_COLS // 2
    new_expert = (i == 0) | (te_ref[i] != te_ref[jnp.maximum(i - 1, 0)])

    @pl.when(new_expert & (i < nu_ref[0]))
    def _():
        for blk in range(2 * D_FF // MXU_COLS):
            wblk = wgu_ref[0, :, blk * MXU_COLS:(blk + 1) * MXU_COLS].astype(BF16)
            pw = jnp.dot(wblk, perm_ref[...], preferred_element_type=F32)
            wg_s[:, blk * half:(blk + 1) * half] = pw[:, :half].astype(BF16)
            wl_s[:, blk * half:(blk + 1) * half] = pw[:, half:].astype(BF16)
        wd_s[...] = wd_ref[0].astype(BF16)

    @pl.when(i < nu_ref[0])
    def _():
        xs = xs_ref[...]
        g = jnp.dot(xs, wg_s[...], preferred_element_type=F32) + bg_ref[0]
        l = jnp.dot(xs, wl_s[...], preferred_element_type=F32) + bl_ref[0]
        g = jnp.minimum(g, SWIGLU_LIMIT)
        l = jnp.clip(l, -SWIGLU_LIMIT, SWIGLU_LIMIT)
        act = g * jax.nn.sigmoid(SWIGLU_ALPHA * g) * (l + 1.0)
        y = jnp.dot(act.astype(BF16), wd_s[...], preferred_element_type=F32) + bd_ref[0]
        y_ref[...] = y.astype(y_ref.dtype)

    @pl.when(i >= nu_ref[0])
    def _():
        y_ref[...] = jnp.zeros_like(y_ref)


def _moe_call(tile_expert, n_used, xs, wgu, bg, bl, wd, bd, *, y_prev=None, base_tile=0, total_rows=None):
    p, d = xs.shape
    n_tiles = p // TM
    total_rows = p if total_rows is None else total_rows
    tile_expert = lax.slice(tile_expert, (base_tile,), (base_tile + n_tiles,))
    n_used = jnp.clip(n_used - base_tile, 1, n_tiles)
    half = MXU_COLS // 2
    r = lax.broadcasted_iota(jnp.int32, (MXU_COLS, MXU_COLS), 0)
    c = lax.broadcasted_iota(jnp.int32, (MXU_COLS, MXU_COLS), 1)
    perm = (r == jnp.where(c < half, 2 * c, 2 * (c - half) + 1)).astype(BF16)
    row = lambda i, te, nu: (jnp.minimum(i, nu[0] - 1), 0)
    ex3 = lambda i, te, nu: (te[i], 0, 0)
    in_specs = [
        pl.BlockSpec((TM, d), row),
        pl.BlockSpec((1, d, 2 * D_FF), ex3),
        pl.BlockSpec((1, 1, D_FF), ex3), pl.BlockSpec((1, 1, D_FF), ex3),
        pl.BlockSpec((1, D_FF, d), ex3), pl.BlockSpec((1, 1, d), ex3),
        pl.BlockSpec((MXU_COLS, MXU_COLS), lambda i, te, nu: (0, 0)),
    ]
    operands = [tile_expert, n_used, xs, wgu, bg, bl, wd, bd, perm]
    aliases = {}
    if y_prev is not None:
        in_specs.append(pl.BlockSpec(memory_space=pl.ANY))
        aliases = {len(operands): 0}
        operands.append(y_prev)
    grid_spec = pltpu.PrefetchScalarGridSpec(
        num_scalar_prefetch=2,
        grid=(n_tiles,),
        in_specs=in_specs,
        out_specs=pl.BlockSpec((TM, d), lambda i, te, nu: (i + base_tile, 0)),
        scratch_shapes=[pltpu.VMEM((d, D_FF), BF16), pltpu.VMEM((d, D_FF), BF16), pltpu.VMEM((D_FF, d), BF16)],
    )
    return pl.pallas_call(
        _moe_kernel,
        grid_spec=grid_spec,
        out_shape=jax.ShapeDtypeStruct((total_rows, d), BF16),
        input_output_aliases=aliases,
        compiler_params=pltpu.CompilerParams(
            dimension_semantics=("arbitrary",), vmem_limit_bytes=VMEM_LIMIT),
        name="moe_experts",
    )(*operands)


def _final_kernel(x1_ref, yg_ref, gw_ref, nw_ref, *rest):
    out_ref = rest[-1]
    gw = gw_ref[...]
    x2 = x1_ref[...]
    for k in range(TOP_K):
        x2 = x2 + gw[:, k:k + 1] * yg_ref[k].astype(F32)
    out_ref[...] = _rms(x2, nw_ref[...])


def _final_call(x1, yg, gw, nw, *, out_prev=None, base_block=0):
    t, d = x1.shape
    blk = pl.BlockSpec((TF, d), lambda i: (i + base_block, 0))
    in_specs = [blk, pl.BlockSpec((TOP_K, TF, d), lambda i: (0, i, 0)),
                pl.BlockSpec((TF, TOP_K), lambda i: (i + base_block, 0)), pl.BlockSpec((1, d), lambda i: (0, 0))]
    operands = [x1, yg, gw, nw]
    aliases = {}
    if out_prev is not None:
        in_specs.append(pl.BlockSpec(memory_space=pl.ANY))
        aliases = {len(operands): 0}
        operands.append(out_prev)
    return pl.pallas_call(
        _final_kernel,
        grid=(yg.shape[1] // TF,),
        in_specs=in_specs,
        out_specs=blk,
        out_shape=jax.ShapeDtypeStruct((t, d), F32),
        input_output_aliases=aliases,
        compiler_params=pltpu.CompilerParams(dimension_semantics=("arbitrary",)),
        name="final_norm",
    )(*operands)


def _chunk_masks():
    r = lax.broadcasted_iota(jnp.int32, (TSH, TSH), 0)
    c = lax.broadcasted_iota(jnp.int32, (TSH, TSH), 1)
    tri = ((r // CHUNK == c // CHUNK) & (r >= c)).astype(BF16)
    r = lax.broadcasted_iota(jnp.int32, (TSL, TSL), 0)
    c = lax.broadcasted_iota(jnp.int32, (TSL, TSL), 1)
    ustr = (r < c).astype(BF16)
    return tri, ustr


def kernel(x, norm_mix, w_in, lb_raw, gn_w, conv_w, conv_b, w_rg_a, b_rg_a, w_rg_x, b_rg_x, lru_lambda,
           w_proj_a, w_proj_b, w_out, norm_ffn, w_router, b_router, w_gu, b_gu, w_down, b_down, norm_final):
    b, s, d = x.shape
    t = b * s
    assert w_in.shape[0] == 1 and s % TSH == 0 and s % TSL == 0
    hk = HA_HEADS * HA_DK
    w = LRU_WIDTH
    tri, ustr = _chunk_masks()

    wi = w_in[0]
    o_lru = 4 * hk
    wa = wi[:, 0:o_lru].astype(BF16)
    wb = jnp.concatenate([wi[:, o_lru:o_lru + 2 * w], wi[:, o_lru + 2 * w + d:o_lru + 2 * w + 2 * d]],
                         axis=1).astype(BF16)
    wga = wi[:, o_lru + 2 * w:o_lru + 2 * w + d].astype(BF16)
    row = lambda v: v.reshape(1, -1)

    ya = _hgrn_call(x, row(norm_mix[0]), wa, wga, lb_raw, row(gn_w[0]), w_proj_a[0].astype(BF16), tri)
    x1, h2, ri, rw, cnt = _lru_call(
        x, ya, row(norm_mix[0]), wb, conv_w[0], row(conv_b[0]), w_rg_a[0].astype(BF16), row(b_rg_a[0]),
        w_rg_x[0].astype(BF16), row(b_rg_x[0]), row(lru_lambda[0]), w_proj_b[0].astype(BF16),
        w_out[0].astype(BF16), row(norm_ffn[0]), w_router[0].T, b_router[0].reshape(-1, 1), ustr)

    idx, pos, gate_w = ri[0:TOP_K], ri[TOP_K:2 * TOP_K], rw[0:TOP_K]
    counts = cnt[:, 0].astype(jnp.int32)
    padded = ((counts + TM - 1) // TM) * TM
    ends = jnp.cumsum(padded)
    offs = ends - padded
    idx_d, slot = idx.reshape(-1, LANES), pos.reshape(-1, LANES)
    for e in range(N_EXPERTS):
        slot = slot + jnp.where(idx_d == e, offs[e], 0)
    slot = slot.reshape(TOP_K, t)
    n_rows = TOP_K * t + N_EXPERTS * TM
    n_tiles = n_rows // TM
    tile_expert = jnp.minimum(
        jnp.sum(jnp.arange(n_tiles, dtype=jnp.int32)[:, None] * TM >= ends[None, :], axis=1),
        N_EXPERTS - 1).astype(jnp.int32)
    n_used = (ends[-1:] // TM).astype(jnp.int32)

    n_pad = padded - counts
    pads_before = jnp.cumsum(n_pad) - n_pad
    j = jnp.arange(TM, dtype=jnp.int32)[None, :]
    e_base = (jnp.arange(N_EXPERTS, dtype=jnp.int32) * TM)[:, None]
    pad_slot = jnp.where(j < n_pad[:, None], (offs + counts)[:, None] + j,
                         ends[-1] + e_base + j - (pads_before + n_pad)[:, None])
    all_slots = jnp.concatenate([slot.reshape(-1), pad_slot.reshape(-1)])
    tok_of_slot = _invert_slots(all_slots, n_rows, t)
    def bounds(total, unit, parts):
        cuts = [total * sum(parts[:k]) // sum(parts) for k in range(len(parts) + 1)]
        assert all(c % unit == 0 for c in cuts)
        return list(zip(cuts[:-1], cuts[1:]))

    h2r, x1r = h2.reshape(t, d), x1.reshape(t, d)
    bgs, bls = b_gu[0][:, 0::2].reshape(N_EXPERTS, 1, D_FF), b_gu[0][:, 1::2].reshape(N_EXPERTS, 1, D_FF)
    bds = b_down[0].reshape(N_EXPERTS, 1, d)
    row_parts = bounds(n_rows, TM, MOE_PARTS)
    assert row_parts[-1][0] < TOP_K * t
    y = None
    for lo, hi in row_parts:
        xs = h2r.at[tok_of_slot[lo:hi]].get(mode="promise_in_bounds")
        y = _moe_call(tile_expert, n_used, xs, w_gu[0], bgs, bls, w_down[0], bds,
                      y_prev=y, base_tile=lo // TM, total_rows=n_rows)

    gw = gate_w.T
    out = None
    for lo, hi in bounds(t, TF, FINAL_PARTS):
        yg = y.at[slot[:, lo:hi]].get(mode="promise_in_bounds")
        out = _final_call(x1r, yg, gw, row(norm_final), out_prev=out, base_block=lo // TF)
    return out.reshape(b, s, d)
```

```python
import functools

import jax
import jax.numpy as jnp
from jax import lax
from jax.experimental import pallas as pl
from jax.experimental.pallas import tpu as pltpu
from jax.experimental.pallas import tpu_sc as plsc

F32 = jnp.float32
BF16 = jnp.bfloat16

D_MODEL = 1024
HA_HEADS = 8
HA_DK = 128
HA_DV = 128
CHUNK = 64
LRU_WIDTH = 1024
LRU_HEADS = 4
LRU_BLOCK = LRU_WIDTH // LRU_HEADS
CONV_WIDTH = 4
RG_C = 8.0
N_EXPERTS = 32
TOP_K = 4
D_FF = 1024
SWIGLU_LIMIT = 7.0
SWIGLU_ALPHA = 1.702
EPS = 1e-5

SUBLANES = 8
LANES = 128
MXU_COLS = 256
TSH = 256
TSL = 512
TM = 512
TF = 512
MOE_PARTS = (1, 3)
FINAL_PARTS = (1, 1, 2)
VMEM_LIMIT = 56 * 1024 * 1024
SC_WINDOW = 128
SC_ROW = 128

NT_DIMS = (((1,), (1,)), ((), ()))
TN_DIMS = (((0,), (0,)), ((), ()))


def _rms(x, w):
    return x * lax.rsqrt(jnp.mean(x * x, axis=-1, keepdims=True) + EPS) * w


def _split3(v):
    hi = v.astype(BF16)
    r1 = v - hi.astype(F32)
    mid = r1.astype(BF16)
    lo = (r1 - mid.astype(F32)).astype(BF16)
    return hi, mid, lo


def _hgrn_kernel(x_ref, nw_ref, wa_ref, wga_ref, lbraw_ref, gnw_ref, wpa_ref, tri_ref, out_ref,
                 st_ref, qin_ref, kin_ref, kdec_ref, v_ref, on_ref):
    n_chunks = TSH // CHUNK
    hk = HA_HEADS * HA_DK

    @pl.when(pl.program_id(1) == 0)
    def _():
        st_ref[...] = jnp.zeros_like(st_ref)

    hb = _rms(x_ref[0], nw_ref[...]).astype(BF16)

    lbr = lbraw_ref[...]
    rows = [lbr[j:j + 1, :] for j in range(lbr.shape[0])]
    mx = functools.reduce(jnp.maximum, rows)
    es = [jnp.exp(r - mx) for r in rows]
    lb = es[0] / functools.reduce(lambda a, b: a + b, es)

    fl = jnp.dot(hb, wa_ref[:, hk:2 * hk], preferred_element_type=F32)
    qs = jax.nn.silu(jnp.dot(hb, wa_ref[:, 0:hk], preferred_element_type=F32))
    v_ref[...] = jnp.dot(hb, wa_ref[:, 2 * hk:3 * hk], preferred_element_type=F32).astype(BF16)
    f = lb + (1.0 - lb) * jax.nn.sigmoid(fl)
    lf = jnp.log(f)
    k = 1.0 - f
    tri = tri_ref[...]
    bcum = functools.reduce(lambda a, b: a + b,
                            [jnp.dot(tri, p, preferred_element_type=F32) for p in _split3(lf)])
    b_last = [bcum[c * CHUNK + CHUNK - 1:c * CHUNK + CHUNK, :] for c in range(n_chunks)]
    bl_full = jnp.concatenate([jnp.broadcast_to(b, (CHUNK, hk)) for b in b_last], axis=0)

    merge_gate = jax.nn.sigmoid(jnp.dot(hb, wga_ref[...], preferred_element_type=F32))
    g = jnp.dot(hb, wa_ref[:, 3 * hk:4 * hk], preferred_element_type=F32)
    out_gate = jax.nn.silu(g) * gnw_ref[...]
    qin_ref[...] = (qs * jnp.exp(bcum)).astype(BF16)
    kin_ref[...] = (k * jnp.exp(-bcum)).astype(BF16)
    kdec_ref[...] = (k * jnp.exp(bl_full - bcum)).astype(BF16)

    causal = (lax.broadcasted_iota(jnp.int32, (CHUNK, CHUNK), 0)
              >= lax.broadcasted_iota(jnp.int32, (CHUNK, CHUNK), 1))
    sts = [st_ref[h] for h in range(HA_HEADS)]
    for c in range(n_chunks):
        r = slice(c * CHUNK, (c + 1) * CHUNK)
        edec = jnp.exp(b_last[c])
        for h in range(HA_HEADS):
            l = slice(h * HA_DK, (h + 1) * HA_DK)
            qh, kh, kd, vh = qin_ref[r, l], kin_ref[r, l], kdec_ref[r, l], v_ref[r, l]
            s = lax.dot_general(qh, kh, NT_DIMS, preferred_element_type=F32)
            s = jnp.where(causal, s, 0.0).astype(BF16)
            o = (jnp.dot(s, vh, preferred_element_type=F32)
                 + jnp.dot(qh, sts[h].T.astype(BF16), preferred_element_type=F32))
            sts[h] = sts[h] * edec[:, l] + lax.dot_general(vh, kd, TN_DIMS, preferred_element_type=F32)
            ms = jnp.mean(o * o, axis=-1, keepdims=True)
            on_ref[r, l] = o * lax.rsqrt(ms + EPS)
    for h in range(HA_HEADS):
        st_ref[h] = sts[h]

    ya = jnp.dot((on_ref[...] * out_gate).astype(BF16), wpa_ref[...], preferred_element_type=F32)
    out_ref[0] = merge_gate * ya


def _hgrn_call(x, nw, wa, wga, lb_raw, gnw, wpa, tri):
    b, s, d = x.shape
    hk = HA_HEADS * HA_DK
    const = lambda shape: pl.BlockSpec(shape, lambda i, j: (0,) * len(shape))
    return pl.pallas_call(
        _hgrn_kernel,
        grid=(b, s // TSH),
        in_specs=[
            pl.BlockSpec((1, TSH, d), lambda i, j: (i, j, 0)),
            const((1, d)), const((d, 4 * hk)), const((d, d)), const(lb_raw.shape), const((1, hk)),
            const((hk, d)), const((TSH, TSH)),
        ],
        out_specs=pl.BlockSpec((1, TSH, d), lambda i, j: (i, j, 0)),
        out_shape=jax.ShapeDtypeStruct((b, s, d), F32),
        scratch_shapes=[
            pltpu.VMEM((HA_HEADS, HA_DV, HA_DK), F32),
            pltpu.VMEM((TSH, hk), BF16), pltpu.VMEM((TSH, hk), BF16), pltpu.VMEM((TSH, hk), BF16),
            pltpu.VMEM((TSH, hk), BF16), pltpu.VMEM((TSH, hk), F32),
        ],
        compiler_params=pltpu.CompilerParams(
            dimension_semantics=("arbitrary", "arbitrary"), vmem_limit_bytes=VMEM_LIMIT),
        name="hgrn",
    )(x, nw, wa, wga, lb_raw, gnw, wpa, tri)


def _lru_kernel(x_ref, ya_ref, nw_ref, wb_ref, cw_ref, cb_ref, wra_ref, bra_ref, wrx_ref, brx_ref, lam_ref,
                wpb_ref, wout_ref, nffn_ref, wrt_ref, brt_ref, ustr_ref,
                x1_ref, h2_ref, ri_ref, rw_ref, cnt_ref,
                xbuf_ref, a_ref, u_ref, hl_ref, pp_ref, hc_ref, carry_ref):
    seg = TSL // SUBLANES
    pitch = seg + SUBLANES
    w = LRU_WIDTH
    bi, si = pl.program_id(0), pl.program_id(1)

    @pl.when(si == 0)
    def _():
        xbuf_ref[0:SUBLANES, :] = jnp.zeros((SUBLANES, w), F32)
        hc_ref[...] = jnp.zeros_like(hc_ref)

    @pl.when((bi == 0) & (si == 0))
    def _():
        carry_ref[...] = jnp.zeros_like(carry_ref)

    x = x_ref[0]
    hb = _rms(x, nw_ref[...]).astype(BF16)

    xb = jnp.dot(hb, wb_ref[:, 0:w], preferred_element_type=F32)
    xbuf_ref[SUBLANES:SUBLANES + TSL, :] = xb
    cw = cw_ref[...]
    xc = cb_ref[...] + cw[CONV_WIDTH - 1:CONV_WIDTH, :] * xb
    for kk in range(CONV_WIDTH - 1):
        shift = CONV_WIDTH - 1 - kk
        xc = xc + cw[kk:kk + 1, :] * xbuf_ref[SUBLANES - shift:SUBLANES - shift + TSL, :]
    xbuf_ref[0:SUBLANES, :] = xbuf_ref[TSL:TSL + SUBLANES, :]

    xcb = xc.astype(BF16)
    r_parts, i_parts = [], []
    for j in range(LRU_HEADS):
        l = slice(j * LRU_BLOCK, (j + 1) * LRU_BLOCK)
        r_parts.append(jnp.dot(xcb[:, l], wra_ref[j], preferred_element_type=F32))
        i_parts.append(jnp.dot(xcb[:, l], wrx_ref[j], preferred_element_type=F32))
    rg = jax.nn.sigmoid(jnp.concatenate(r_parts, axis=1) + bra_ref[...])
    ig = jax.nn.sigmoid(jnp.concatenate(i_parts, axis=1) + brx_ref[...])
    log_a = RG_C * rg * jax.nn.log_sigmoid(lam_ref[...])
    a = jnp.exp(log_a)
    m2 = 1.0 - a * a
    mult = jnp.where(m2 > 0.0, m2 * lax.rsqrt(m2), 0.0)
    is_first = (lax.broadcasted_iota(jnp.int32, (TSL, 1), 0) == 0) & (si == 0)
    mult = jnp.where(is_first, 1.0, mult)
    u = xc * ig * mult
    gelu_gate = jax.nn.gelu(jnp.dot(hb, wb_ref[:, w:2 * w], preferred_element_type=F32))
    merge_gate = jax.nn.sigmoid(jnp.dot(hb, wb_ref[:, 2 * w:3 * w], preferred_element_type=F32))

    n_groups = w // LANES
    group_rows = SUBLANES * pitch
    for gi in range(n_groups):
        for j in range(SUBLANES):
            dst = slice(gi * group_rows + j * pitch, gi * group_rows + j * pitch + seg)
            a_ref[dst, :] = a[j * seg:(j + 1) * seg, gi * LANES:(gi + 1) * LANES]
            u_ref[dst, :] = u[j * seg:(j + 1) * seg, gi * LANES:(gi + 1) * LANES]

    def step(i, hp):
        hs, ps = hp
        new_h, new_p = [], []
        for gi in range(n_groups):
            rows = pl.ds(gi * group_rows + i, SUBLANES, stride=pitch)
            ai = a_ref[rows, :]
            h = ai * hs[gi] + u_ref[rows, :]
            p = ai * ps[gi]
            hl_ref[rows, :] = h
            pp_ref[rows, :] = p
            new_h.append(h)
            new_p.append(p)
        return tuple(new_h), tuple(new_p)

    init = (tuple(jnp.zeros((SUBLANES, LANES), F32) for _ in range(n_groups)),
            tuple(jnp.ones((SUBLANES, LANES), F32) for _ in range(n_groups)))
    h_end, p_end = lax.fori_loop(0, seg, step, init, unroll=True)
    h_end = jnp.concatenate(h_end, axis=1)
    p_end = jnp.concatenate(p_end, axis=1)
    c = hc_ref[...]
    h_parts = []
    for j in range(SUBLANES):
        src = [slice(gi * group_rows + j * pitch, gi * group_rows + j * pitch + seg) for gi in range(n_groups)]
        hl = jnp.concatenate([hl_ref[r, :] for r in src], axis=1)
        pp = jnp.concatenate([pp_ref[r, :] for r in src], axis=1)
        h_parts.append(hl + pp * c)
        c = h_end[j:j + 1, :] + p_end[j:j + 1, :] * c
    hc_ref[...] = c
    x1_halves = []
    for hf in range(2):
        rows = slice(hf * (TSL // 2), (hf + 1) * (TSL // 2))
        h_half = jnp.concatenate(h_parts[hf * (SUBLANES // 2):(hf + 1) * (SUBLANES // 2)], axis=0)
        ob = (h_half * gelu_gate[rows]).astype(BF16)
        yb = jnp.dot(ob, wpb_ref[...], preferred_element_type=F32)
        merged = ya_ref[0, rows, :] + merge_gate[rows] * yb
        x1_halves.append(x[rows] + jnp.dot(merged.astype(BF16), wout_ref[...], preferred_element_type=F32))
    x1 = jnp.concatenate(x1_halves, axis=0)
    x1_ref[0] = x1

    h2 = _rms(x1, nffn_ref[...])
    h_hi = h2.astype(BF16)
    h2_ref[0] = h_hi
    h_lo = (h2 - h_hi.astype(F32)).astype(BF16)
    wt = wrt_ref[...]
    w_hi = wt.astype(BF16)
    w_lo = (wt - w_hi.astype(F32)).astype(BF16)
    both = lax.dot_general(jnp.concatenate([w_hi, w_lo], axis=0), h_hi, NT_DIMS, preferred_element_type=F32)
    logits = (both[:N_EXPERTS] + both[N_EXPERTS:]
              + lax.dot_general(w_hi, h_lo, NT_DIMS, preferred_element_type=F32)) + brt_ref[...]

    eio = lax.broadcasted_iota(jnp.int32, (N_EXPERTS, TSL), 0)
    idxs, vals = [], []
    for _ in range(TOP_K):
        m = jnp.max(logits, axis=0, keepdims=True)
        ik = jnp.min(jnp.where(logits == m, eio, N_EXPERTS), axis=0, keepdims=True)
        idxs.append(ik)
        vals.append(m)
        logits = jnp.where(eio == ik, -jnp.inf, logits)
    exps = [jnp.exp(v - vals[0]) for v in vals]
    den = functools.reduce(lambda p, q: p + q, exps)
    onehots = [(eio == ik) for ik in idxs]
    oh = functools.reduce(lambda p, q: p + q, [o.astype(F32) for o in onehots])
    cum = jnp.dot(oh.astype(BF16), ustr_ref[...], preferred_element_type=F32) + carry_ref[:, 0:1]
    poss = [jnp.sum(jnp.where(o, cum, 0.0), axis=0, keepdims=True).astype(jnp.int32) for o in onehots]
    carry_ref[...] = carry_ref[...] + jnp.sum(oh, axis=1, keepdims=True)
    ri_ref[...] = jnp.concatenate(idxs + poss, axis=0)
    rw_ref[...] = jnp.concatenate([e / den for e in exps] + [jnp.zeros((SUBLANES - TOP_K, TSL), F32)], axis=0)
    cnt_ref[...] = carry_ref[...]


def _lru_call(x, ya, nw, wb, cw, cb, wra, bra, wrx, brx, lam, wpb, wout, nffn, wrt, brt, ustr):
    b, s, d = x.shape
    w = LRU_WIDTH
    scan_shape = ((TSL + SUBLANES * SUBLANES) * (w // LANES), LANES)
    const = lambda shape: pl.BlockSpec(shape, lambda i, j: (0,) * len(shape))
    tok = pl.BlockSpec((1, TSL, d), lambda i, j: (i, j, 0))
    rt = pl.BlockSpec((SUBLANES, TSL), lambda i, j: (0, i * (s // TSL) + j))
    return pl.pallas_call(
        _lru_kernel,
        grid=(b, s // TSL),
        in_specs=[
            tok, tok, const((1, d)), const((d, 3 * w)), const((CONV_WIDTH, w)), const((1, w)),
            const((LRU_HEADS, LRU_BLOCK, LRU_BLOCK)), const((1, w)),
            const((LRU_HEADS, LRU_BLOCK, LRU_BLOCK)), const((1, w)), const((1, w)),
            const((w, d)), const((d, d)), const((1, d)), const((N_EXPERTS, d)), const((N_EXPERTS, 1)),
            const((TSL, TSL)),
        ],
        out_specs=[tok, tok, rt, rt, const((N_EXPERTS, LANES))],
        out_shape=[
            jax.ShapeDtypeStruct((b, s, d), F32),
            jax.ShapeDtypeStruct((b, s, d), BF16),
            jax.ShapeDtypeStruct((SUBLANES, b * s), jnp.int32),
            jax.ShapeDtypeStruct((SUBLANES, b * s), F32),
            jax.ShapeDtypeStruct((N_EXPERTS, LANES), F32),
        ],
        scratch_shapes=[
            pltpu.VMEM((TSL + SUBLANES, w), F32),
            pltpu.VMEM(scan_shape, F32), pltpu.VMEM(scan_shape, F32),
            pltpu.VMEM(scan_shape, F32), pltpu.VMEM(scan_shape, F32),
            pltpu.VMEM((1, w), F32), pltpu.VMEM((N_EXPERTS, LANES), F32),
        ],
        compiler_params=pltpu.CompilerParams(
            dimension_semantics=("arbitrary", "arbitrary"), vmem_limit_bytes=VMEM_LIMIT),
        name="lru_router",
    )(x, ya, nw, wb, cw, cb, wra, bra, wrx, brx, lam, wpb, wout, nffn, wrt, brt, ustr)


def _invert_slots(slot_flat, n_rows, t):
    n = slot_flat.shape[0]
    vals = jnp.broadcast_to((jnp.arange(n, dtype=jnp.int32) % t)[:, None], (n, SC_ROW))
    mesh = plsc.VectorSubcoreMesh(core_axis_name="core", subcore_axis_name="subcore")

    @functools.partial(pl.kernel, out_type=jax.ShapeDtypeStruct((n_rows, SC_ROW), jnp.int32), mesh=mesh,
                       scratch_types=[])
    def scatter_kernel(x_hbm, i_hbm, o_hbm):
        def body(x_vmem, i_vmem):
            pltpu.sync_copy(x_vmem, o_hbm.at[i_vmem.at[0]])

        pltpu.emit_pipeline(
            body,
            grid=(n // SC_WINDOW,),
            in_specs=[pl.BlockSpec((SC_WINDOW, SC_ROW), lambda i: (i, 0)),
                      pl.BlockSpec((1, SC_WINDOW), lambda i: (0, i))],
            out_specs=[],
            core_axis_name=("core", "subcore"),
            dimension_semantics=(pltpu.PARALLEL,),
        )(x_hbm, i_hbm)

    return scatter_kernel(vals, slot_flat.reshape(1, n))[:, 0]


def _moe_kernel(te_ref, nu_ref, xs_ref, wgu_ref, bg_ref, bl_ref, wd_ref, bd_ref, perm_ref, *rest):
    y_ref, wg_s, wl_s, wd_s = rest[-4:]
    i = pl.program_id(0)
    half = MXU_COLS // 2
    new_expert = (i == 0) | (te_ref[i] != te_ref[jnp.maximum(i - 1, 0)])

    @pl.when(new_expert & (i < nu_ref[0]))
    def _():
        for blk in range(2 * D_FF // MXU_COLS):
            wblk = wgu_ref[0, :, blk * MXU_COLS:(blk + 1) * MXU_COLS].astype(BF16)
            pw = jnp.dot(wblk, perm_ref[...], preferred_element_type=F32)
            wg_s[:, blk * half:(blk + 1) * half] = pw[:, :half].astype(BF16)
            wl_s[:, blk * half:(blk + 1) * half] = pw[:, half:].astype(BF16)
        wd_s[...] = wd_ref[0].astype(BF16)

    @pl.when(i < nu_ref[0])
    def _():
        xs = xs_ref[...]
        g = jnp.dot(xs, wg_s[...], preferred_element_type=F32) + bg_ref[0]
        l = jnp.dot(xs, wl_s[...], preferred_element_type=F32) + bl_ref[0]
        g = jnp.minimum(g, SWIGLU_LIMIT)
        l = jnp.clip(l, -SWIGLU_LIMIT, SWIGLU_LIMIT)
        act = g * jax.nn.sigmoid(SWIGLU_ALPHA * g) * (l + 1.0)
        y = jnp.dot(act.astype(BF16), wd_s[...], preferred_element_type=F32) + bd_ref[0]
        y_ref[...] = y.astype(y_ref.dtype)

    @pl.when(i >= nu_ref[0])
    def _():
        y_ref[...] = jnp.zeros_like(y_ref)


def _moe_call(tile_expert, n_used, xs, wgu, bg, bl, wd, bd, *, y_prev=None, base_tile=0, total_rows=None):
    p, d = xs.shape
    n_tiles = p // TM
    total_rows = p if total_rows is None else total_rows
    tile_expert = lax.slice(tile_expert, (base_tile,), (base_tile + n_tiles,))
    n_used = jnp.clip(n_used - base_tile, 1, n_tiles)
    half = MXU_COLS // 2
    r = lax.broadcasted_iota(jnp.int32, (MXU_COLS, MXU_COLS), 0)
    c = lax.broadcasted_iota(jnp.int32, (MXU_COLS, MXU_COLS), 1)
    perm = (r == jnp.where(c < half, 2 * c, 2 * (c - half) + 1)).astype(BF16)
    row = lambda i, te, nu: (jnp.minimum(i, nu[0] - 1), 0)
    ex3 = lambda i, te, nu: (te[i], 0, 0)
    in_specs = [
        pl.BlockSpec((TM, d), row),
        pl.BlockSpec((1, d, 2 * D_FF), ex3),
        pl.BlockSpec((1, 1, D_FF), ex3), pl.BlockSpec((1, 1, D_FF), ex3),
        pl.BlockSpec((1, D_FF, d), ex3), pl.BlockSpec((1, 1, d), ex3),
        pl.BlockSpec((MXU_COLS, MXU_COLS), lambda i, te, nu: (0, 0)),
    ]
    operands = [tile_expert, n_used, xs, wgu, bg, bl, wd, bd, perm]
    aliases = {}
    if y_prev is not None:
        in_specs.append(pl.BlockSpec(memory_space=pl.ANY))
        aliases = {len(operands): 0}
        operands.append(y_prev)
    grid_spec = pltpu.PrefetchScalarGridSpec(
        num_scalar_prefetch=2,
        grid=(n_tiles,),
        in_specs=in_specs,
        out_specs=pl.BlockSpec((TM, d), lambda i, te, nu: (i + base_tile, 0)),
        scratch_shapes=[pltpu.VMEM((d, D_FF), BF16), pltpu.VMEM((d, D_FF), BF16), pltpu.VMEM((D_FF, d), BF16)],
    )
    return pl.pallas_call(
        _moe_kernel,
        grid_spec=grid_spec,
        out_shape=jax.ShapeDtypeStruct((total_rows, d), BF16),
        input_output_aliases=aliases,
        compiler_params=pltpu.CompilerParams(
            dimension_semantics=("arbitrary",), vmem_limit_bytes=VMEM_LIMIT),
        name="moe_experts",
    )(*operands)


def _final_kernel(x1_ref, yg_ref, gw_ref, nw_ref, *rest):
    out_ref = rest[-1]
    gw = gw_ref[...]
    x2 = x1_ref[...]
    for k in range(TOP_K):
        x2 = x2 + gw[:, k:k + 1] * yg_ref[k].astype(F32)
    out_ref[...] = _rms(x2, nw_ref[...])


def _final_call(x1, yg, gw, nw, *, out_prev=None, base_block=0):
    t, d = x1.shape
    blk = pl.BlockSpec((TF, d), lambda i: (i + base_block, 0))
    in_specs = [blk, pl.BlockSpec((TOP_K, TF, d), lambda i: (0, i, 0)),
                pl.BlockSpec((TF, TOP_K), lambda i: (i + base_block, 0)), pl.BlockSpec((1, d), lambda i: (0, 0))]
    operands = [x1, yg, gw, nw]
    aliases = {}
    if out_prev is not None:
        in_specs.append(pl.BlockSpec(memory_space=pl.ANY))
        aliases = {len(operands): 0}
        operands.append(out_prev)
    return pl.pallas_call(
        _final_kernel,
        grid=(yg.shape[1] // TF,),
        in_specs=in_specs,
        out_specs=blk,
        out_shape=jax.ShapeDtypeStruct((t, d), F32),
        input_output_aliases=aliases,
        compiler_params=pltpu.CompilerParams(dimension_semantics=("arbitrary",)),
        name="final_norm",
    )(*operands)


def _chunk_masks():
    r = lax.broadcasted_iota(jnp.int32, (TSH, TSH), 0)
    c = lax.broadcasted_iota(jnp.int32, (TSH, TSH), 1)
    tri = ((r // CHUNK == c // CHUNK) & (r >= c)).astype(BF16)
    r = lax.broadcasted_iota(jnp.int32, (TSL, TSL), 0)
    c = lax.broadcasted_iota(jnp.int32, (TSL, TSL), 1)
    ustr = (r < c).astype(BF16)
    return tri, ustr


def kernel(x, norm_mix, w_in, lb_raw, gn_w, conv_w, conv_b, w_rg_a, b_rg_a, w_rg_x, b_rg_x, lru_lambda,
           w_proj_a, w_proj_b, w_out, norm_ffn, w_router, b_router, w_gu, b_gu, w_down, b_down, norm_final):
    b, s, d = x.shape
    t = b * s
    assert w_in.shape[0] == 1 and s % TSH == 0 and s % TSL == 0
    hk = HA_HEADS * HA_DK
    w = LRU_WIDTH
    tri, ustr = _chunk_masks()

    wi = w_in[0]
    o_lru = 4 * hk
    wa = wi[:, 0:o_lru].astype(BF16)
    wb = jnp.concatenate([wi[:, o_lru:o_lru + 2 * w], wi[:, o_lru + 2 * w + d:o_lru + 2 * w + 2 * d]],
                         axis=1).astype(BF16)
    wga = wi[:, o_lru + 2 * w:o_lru + 2 * w + d].astype(BF16)
    row = lambda v: v.reshape(1, -1)

    ya = _hgrn_call(x, row(norm_mix[0]), wa, wga, lb_raw, row(gn_w[0]), w_proj_a[0].astype(BF16), tri)
    x1, h2, ri, rw, cnt = _lru_call(
        x, ya, row(norm_mix[0]), wb, conv_w[0], row(conv_b[0]), w_rg_a[0].astype(BF16), row(b_rg_a[0]),
        w_rg_x[0].astype(BF16), row(b_rg_x[0]), row(lru_lambda[0]), w_proj_b[0].astype(BF16),
        w_out[0].astype(BF16), row(norm_ffn[0]), w_router[0].T, b_router[0].reshape(-1, 1), ustr)

    idx, pos, gate_w = ri[0:TOP_K], ri[TOP_K:2 * TOP_K], rw[0:TOP_K]
    counts = cnt[:, 0].astype(jnp.int32)
    padded = ((counts + TM - 1) // TM) * TM
    ends = jnp.cumsum(padded)
    offs = ends - padded
    idx_d, slot = idx.reshape(-1, LANES), pos.reshape(-1, LANES)
    for e in range(N_EXPERTS):
        slot = slot + jnp.where(idx_d == e, offs[e], 0)
    slot = slot.reshape(TOP_K, t)
    n_rows = TOP_K * t + N_EXPERTS * TM
    n_tiles = n_rows // TM
    tile_expert = jnp.minimum(
        jnp.sum(jnp.arange(n_tiles, dtype=jnp.int32)[:, None] * TM >= ends[None, :], axis=1),
        N_EXPERTS - 1).astype(jnp.int32)
    n_used = (ends[-1:] // TM).astype(jnp.int32)

    n_pad = padded - counts
    pads_before = jnp.cumsum(n_pad) - n_pad
    j = jnp.arange(TM, dtype=jnp.int32)[None, :]
    e_base = (jnp.arange(N_EXPERTS, dtype=jnp.int32) * TM)[:, None]
    pad_slot = jnp.where(j < n_pad[:, None], (offs + counts)[:, None] + j,
                         ends[-1] + e_base + j - (pads_before + n_pad)[:, None])
    all_slots = jnp.concatenate([slot.reshape(-1), pad_slot.reshape(-1)])
    tok_of_slot = _invert_slots(all_slots, n_rows, t)
    def bounds(total, unit, parts):
        cuts = [total * sum(parts[:k]) // sum(parts) for k in range(len(parts) + 1)]
        assert all(c % unit == 0 for c in cuts)
        return list(zip(cuts[:-1], cuts[1:]))

    h2r, x1r = h2.reshape(t, d), x1.reshape(t, d)
    bgs, bls = b_gu[0][:, 0::2].reshape(N_EXPERTS, 1, D_FF), b_gu[0][:, 1::2].reshape(N_EXPERTS, 1, D_FF)
    bds = b_down[0].reshape(N_EXPERTS, 1, d)
    row_parts = bounds(n_rows, TM, MOE_PARTS)
    assert row_parts[-1][0] < TOP_K * t
    y = None
    for lo, hi in row_parts:
        xs = h2r.at[tok_of_slot[lo:hi]].get(mode="promise_in_bounds")
        y = _moe_call(tile_expert, n_used, xs, w_gu[0], bgs, bls, w_down[0], bds,
                      y_prev=y, base_tile=lo // TM, total_rows=n_rows)

    gw = gate_w.T
    out = None
    for lo, hi in bounds(t, TF, FINAL_PARTS):
        yg = y.at[slot[:, lo:hi]].get(mode="promise_in_bounds")
        out = _final_call(x1r, yg, gw, row(norm_final), out_prev=out, base_block=lo // TF)
    return out.reshape(b, s, d)
```
